```python
import jax, jax.numpy as jnp
from jax import lax
import numpy as np

D_MODEL = 2048
BATCH = 8
SEQ = 2048
DEPTH = 2

HEAD_DIM = 128
QBLOCK = 128
GRID_W = 64
EPS = 1e-6
NEG_INF = -1e30

MLA_HEADS = 8
MLA_Q_RANK = 512
MLA_KV_RANK = 256
MLA_NOPE = 128
MLA_ROPE = 64
MLA_V = 128
MLA_ROPE_THETA = 10000.0

GQA_HEADS = 8
GQA_KV_HEADS = 2
GQA_GROUP = GQA_HEADS // GQA_KV_HEADS
AXIAL_THETA = 10000.0

DIL_PATTERNS = ((128, 1), (512, 4), (2048, 16))
DIL_HEADS_PER_GROUP = 4
DIL_HEADS = DIL_HEADS_PER_GROUP * len(DIL_PATTERNS)
PARTIAL_ROPE_DIM = HEAD_DIM // 4
PARTIAL_ROPE_THETA = 500000.0

N_BRANCHES = 3
D_FF = 4 * D_MODEL

A_COLS = MLA_Q_RANK + MLA_KV_RANK + MLA_ROPE
B_COLS = (GQA_HEADS + 2 * GQA_KV_HEADS) * HEAD_DIM
C_COLS = 3 * DIL_HEADS * HEAD_DIM
GATE_COLS = N_BRANCHES * D_MODEL
IN_COLS = A_COLS + B_COLS + C_COLS + GATE_COLS

kernel_name = "hybrid_gated_mla_axialgqa_dilated_encoder"


def rms_norm(x, gain):
    xf = x.astype(jnp.float32)
    y = xf * lax.rsqrt(jnp.mean(xf * xf, axis=-1, keepdims=True) + EPS)
    return (y * gain.astype(jnp.float32)).astype(x.dtype)


def rotary(x, pos, theta):
    half = x.shape[-1] // 2
    inv = theta ** (-jnp.arange(half, dtype=jnp.float32) / half)
    ang = pos[:, None] * inv[None, :]
    cos = jnp.cos(ang)[:, None, :]
    sin = jnp.sin(ang)[:, None, :]
    xf = x.astype(jnp.float32)
    x1, x2 = xf[..., :half], xf[..., half:]
    return jnp.concatenate([x1 * cos - x2 * sin, x1 * sin + x2 * cos], axis=-1).astype(x.dtype)


def dense_block_attention(q, k, v):
    b, s, hkv, g, dk = q.shape
    scale = dk ** -0.5
    nb = s // QBLOCK
    qb = jnp.moveaxis(q.reshape(b, nb, QBLOCK, hkv, g, dk), 1, 0)

    def attend(qblk):
        sc = jnp.einsum('bqhgd,bkhd->bhgqk', qblk, k).astype(jnp.float32) * scale
        p = jax.nn.softmax(sc, axis=-1).astype(v.dtype)
        return jnp.einsum('bhgqk,bkhd->bqhgd', p, v)

    o = lax.map(attend, qb)
    return jnp.moveaxis(o, 0, 1).reshape(b, s, hkv * g, v.shape[-1])


def mla_mixer(xa, pos, q_lat_norm, w_uq, kv_lat_norm, w_ukv, q_head_norm, k_head_norm):
    b, s, _ = xa.shape
    c_q = xa[..., :MLA_Q_RANK]
    c_kv = xa[..., MLA_Q_RANK:MLA_Q_RANK + MLA_KV_RANK]
    k_pe = xa[..., MLA_Q_RANK + MLA_KV_RANK:]
    q = (rms_norm(c_q, q_lat_norm) @ w_uq).reshape(b, s, MLA_HEADS, MLA_NOPE + MLA_ROPE)
    kv = (rms_norm(c_kv, kv_lat_norm) @ w_ukv).reshape(b, s, MLA_HEADS, MLA_NOPE + MLA_V)
    k_nope, v = kv[..., :MLA_NOPE], kv[..., MLA_NOPE:]
    k = jnp.concatenate(
        [k_nope, jnp.broadcast_to(k_pe[:, :, None, :], (b, s, MLA_HEADS, MLA_ROPE))], axis=-1)
    q = rms_norm(q, q_head_norm)
    k = rms_norm(k, k_head_norm)
    q = jnp.concatenate([q[..., :MLA_NOPE], rotary(q[..., MLA_NOPE:], pos, MLA_ROPE_THETA)], axis=-1)
    k = jnp.concatenate([k[..., :MLA_NOPE], rotary(k[..., MLA_NOPE:], pos, MLA_ROPE_THETA)], axis=-1)
    o = dense_block_attention(q[:, :, :, None, :], k, v)
    return o.reshape(b, s, MLA_HEADS * MLA_V)


def axial_rotary(x, row, col):
    half = x.shape[-1] // 2
    return jnp.concatenate([rotary(x[..., :half], row, AXIAL_THETA),
                            rotary(x[..., half:], col, AXIAL_THETA)], axis=-1)


def gqa_mixer(xb, row, col, q_norm, k_norm):
    b, s, _ = xb.shape
    nq = GQA_HEADS * HEAD_DIM
    nk = GQA_KV_HEADS * HEAD_DIM
    q = xb[..., :nq].reshape(b, s, GQA_HEADS, HEAD_DIM)
    k = xb[..., nq:nq + nk].reshape(b, s, GQA_KV_HEADS, HEAD_DIM)
    v = xb[..., nq + nk:].reshape(b, s, GQA_KV_HEADS, HEAD_DIM)
    q = axial_rotary(rms_norm(q, q_norm), row, col)
    k = axial_rotary(rms_norm(k, k_norm), row, col)
    q = q.reshape(b, s, GQA_KV_HEADS, GQA_GROUP, HEAD_DIM)
    o = dense_block_attention(q, k, v)
    return o.reshape(b, s, GQA_HEADS * HEAD_DIM)


def dilated_window_attention(q, k, v, dilation, radius):
    b, s, h, d = q.shape
    scale = d ** -0.5
    L = s // dilation

    def strided(t):
        return t.reshape(b, L, dilation, h, d).transpose(0, 2, 1, 3, 4)

    qd, kd, vd = strided(q), strided(k), strided(v)
    nb = -(-L // QBLOCK)
    lq = nb * QBLOCK
    kb_len = QBLOCK + 2 * radius
    qp = jnp.pad(qd, ((0, 0), (0, 0), (0, lq - L), (0, 0), (0, 0)))
    pad_kv = ((0, 0), (0, 0), (radius, lq - L + radius), (0, 0), (0, 0))
    kp = jnp.pad(kd, pad_kv)
    vp = jnp.pad(vd, pad_kv)
    idx = jnp.arange(nb)[:, None] * QBLOCK + jnp.arange(kb_len)[None, :]
    kb = kp[:, :, idx]
    vb = vp[:, :, idx]
    qb = qp.reshape(b, dilation, nb, QBLOCK, h, d)
    sc = jnp.einsum('brnqhd,brnkhd->brnhqk', qb, kb).astype(jnp.float32) * scale
    rel = jnp.arange(kb_len)[None, :] - radius - jnp.arange(QBLOCK)[:, None]
    korig = idx - radius
    valid = (jnp.abs(rel) <= radius)[None] & ((korig >= 0) & (korig < L))[:, None, :]
    sc = jnp.where(valid[:, None], sc, NEG_INF)
    lse = jax.nn.logsumexp(sc, axis=-1)
    p = jnp.exp(sc - lse[..., None]).astype(v.dtype)
    o = jnp.einsum('brnhqk,brnkhd->brnqhd', p, vb)
    o = o.reshape(b, dilation, lq, h, d)[:, :, :L].transpose(0, 2, 1, 3, 4).reshape(b, s, h, d)
    lse = lse.transpose(0, 1, 2, 4, 3).reshape(b, dilation, lq, h)[:, :, :L]
    lse = lse.transpose(0, 2, 1, 3).reshape(b, s, h)
    return o, lse


def dilated_mixer(xc, pos, q_norm, k_norm):
    b, s, _ = xc.shape
    w = DIL_HEADS * HEAD_DIM
    q = xc[..., :w].reshape(b, s, DIL_HEADS, HEAD_DIM)
    k = xc[..., w:2 * w].reshape(b, s, DIL_HEADS, HEAD_DIM)
    v = xc[..., 2 * w:].reshape(b, s, DIL_HEADS, HEAD_DIM)
    q = rms_norm(q, q_norm)
    k = rms_norm(k, k_norm)
    q = jnp.concatenate([rotary(q[..., :PARTIAL_ROPE_DIM], pos, PARTIAL_ROPE_THETA),
                         q[..., PARTIAL_ROPE_DIM:]], axis=-1)
    k = jnp.concatenate([rotary(k[..., :PARTIAL_ROPE_DIM], pos, PARTIAL_ROPE_THETA),
                         k[..., PARTIAL_ROPE_DIM:]], axis=-1)
    outs, lses = [], []
    for gi, (window, dilation) in enumerate(DIL_PATTERNS):
        hs = slice(gi * DIL_HEADS_PER_GROUP, (gi + 1) * DIL_HEADS_PER_GROUP)
        o, lse = dilated_window_attention(q[:, :, hs], k[:, :, hs], v[:, :, hs],
                                          dilation, window // (2 * dilation))
        outs.append(o)
        lses.append(lse)
    wts = jax.nn.softmax(jnp.stack(lses, axis=0), axis=0)
    o = jnp.sum(wts[..., None] * jnp.stack(outs, axis=0).astype(jnp.float32), axis=0)
    return o.astype(xc.dtype).reshape(b, s, DIL_HEADS_PER_GROUP * HEAD_DIM)


def _normal(key, shape, scale):
    return scale * jax.random.normal(key, shape, jnp.float32)


def _gain(key, shape):
    return 1.0 + 0.05 * jax.random.normal(key, shape, jnp.float32)


def setup_inputs(seed: int = 0) -> dict:
    key = jax.random.key(seed)
    ks = jax.random.split(key, 24)
    return {
        "x": _normal(ks[0], (BATCH, SEQ, D_MODEL), 1.0),
        "attn_norm": _gain(ks[1], (DEPTH, D_MODEL)),
        "w_in": _normal(ks[2], (DEPTH, D_MODEL, IN_COLS), D_MODEL ** -0.5),
        "b_gate": _normal(ks[3], (DEPTH, GATE_COLS), 0.01),
        "mla_q_lat_norm": _gain(ks[4], (DEPTH, MLA_Q_RANK)),
        "w_uq": _normal(ks[5], (DEPTH, MLA_Q_RANK, MLA_HEADS * (MLA_NOPE + MLA_ROPE)), MLA_Q_RANK ** -0.5),
        "mla_kv_lat_norm": _gain(ks[6], (DEPTH, MLA_KV_RANK)),
        "w_ukv": _normal(ks[7], (DEPTH, MLA_KV_RANK, MLA_HEADS * (MLA_NOPE + MLA_V)), MLA_KV_RANK ** -0.5),
        "mla_q_head_norm": _gain(ks[8], (DEPTH, MLA_NOPE + MLA_ROPE)),
        "mla_k_head_norm": _gain(ks[9], (DEPTH, MLA_NOPE + MLA_ROPE)),
        "gqa_q_norm": _gain(ks[10], (DEPTH, HEAD_DIM)),
        "gqa_k_norm": _gain(ks[11], (DEPTH, HEAD_DIM)),
        "dil_q_norm": _gain(ks[12], (DEPTH, HEAD_DIM)),
        "dil_k_norm": _gain(ks[13], (DEPTH, HEAD_DIM)),
        "w_oa": _normal(ks[14], (DEPTH, MLA_HEADS * MLA_V, D_MODEL), (MLA_HEADS * MLA_V) ** -0.5),
        "w_ob": _normal(ks[15], (DEPTH, GQA_HEADS * HEAD_DIM, D_MODEL), (GQA_HEADS * HEAD_DIM) ** -0.5),
        "w_oc": _normal(ks[16], (DEPTH, DIL_HEADS_PER_GROUP * HEAD_DIM, D_MODEL),
                        (DIL_HEADS_PER_GROUP * HEAD_DIM) ** -0.5),
        "w_out": _normal(ks[17], (DEPTH, D_MODEL, D_MODEL), D_MODEL ** -0.5),
        "mlp_norm": _gain(ks[18], (DEPTH, D_MODEL)),
        "w_up": _normal(ks[19], (DEPTH, D_MODEL, D_FF), D_MODEL ** -0.5),
        "w_down": _normal(ks[20], (DEPTH, D_FF, D_MODEL), D_FF ** -0.5),
    }


def reference(x, attn_norm, w_in, b_gate, mla_q_lat_norm, w_uq, mla_kv_lat_norm, w_ukv,
              mla_q_head_norm, mla_k_head_norm, gqa_q_norm, gqa_k_norm, dil_q_norm, dil_k_norm,
              w_oa, w_ob, w_oc, w_out, mlp_norm, w_up, w_down):
    b, s, _ = x.shape
    rows = s // GRID_W
    pos = jnp.arange(s, dtype=jnp.float32)
    row = jnp.repeat(jnp.arange(rows, dtype=jnp.float32), GRID_W)
    col = jnp.tile(jnp.arange(GRID_W, dtype=jnp.float32), rows)
    for l in range(DEPTH):
        xn = rms_norm(x, attn_norm[l])
        proj = xn @ w_in[l]
        xa = proj[..., :A_COLS]
        xb = proj[..., A_COLS:A_COLS + B_COLS]
        xc = proj[..., A_COLS + B_COLS:A_COLS + B_COLS + C_COLS]
        gl = proj[..., A_COLS + B_COLS + C_COLS:] + b_gate[l]
        ya = mla_mixer(xa, pos, mla_q_lat_norm[l], w_uq[l], mla_kv_lat_norm[l], w_ukv[l],
                       mla_q_head_norm[l], mla_k_head_norm[l]) @ w_oa[l]
        yb = gqa_mixer(xb, row, col, gqa_q_norm[l], gqa_k_norm[l]) @ w_ob[l]
        yc = dilated_mixer(xc, pos, dil_q_norm[l], dil_k_norm[l]) @ w_oc[l]
        gates = jax.nn.sigmoid(gl.astype(jnp.float32)).astype(x.dtype).reshape(b, s, N_BRANCHES, D_MODEL)
        merged = gates[:, :, 0] * ya + gates[:, :, 1] * yb + gates[:, :, 2] * yc
        x = x + merged @ w_out[l]
        hn = rms_norm(x, mlp_norm[l])
        x = x + jnp.square(jax.nn.relu(hn @ w_up[l])) @ w_down[l]
    return x
```

```python
import functools

import jax
import jax.numpy as jnp
from jax import lax
from jax.experimental import pallas as pl
from jax.experimental.pallas import tpu as pltpu

F32 = jnp.float32
BF16 = jnp.bfloat16

LANES = 128
VMEM_CAP_BYTES = 60000 * 1024

HEAD_DIM = 128
GRID_W = 64
EPS = 1e-6
NEG_INF = -1e30

MLA_HEADS = 8
MLA_Q_RANK = 512
MLA_KV_RANK = 256
MLA_NOPE = 128
MLA_ROPE = 64
MLA_V = 128
MLA_ROPE_THETA = 10000.0
MLA_QK = MLA_NOPE + MLA_ROPE
MLA_QK_PAD = 2 * LANES

GQA_HEADS = 8
GQA_KV_HEADS = 2
GQA_GROUP = GQA_HEADS // GQA_KV_HEADS
AXIAL_THETA = 10000.0

DIL_PATTERNS = ((128, 1), (512, 4), (2048, 16))
DIL_HPG = 4
DIL_HEADS = DIL_HPG * len(DIL_PATTERNS)
PARTIAL_ROPE_DIM = HEAD_DIM // 4
PARTIAL_ROPE_THETA = 500000.0
QBLOCK = 128

A_COLS = MLA_Q_RANK + MLA_KV_RANK + MLA_ROPE
B_COLS = (GQA_HEADS + 2 * GQA_KV_HEADS) * HEAD_DIM
C_COLS = 3 * DIL_HEADS * HEAD_DIM


def _round_up(n, m):
    return -(-n // m) * m


def _params(semantics, *block_bytes, scratch_bytes=0):
    need = 2 * sum(block_bytes) + scratch_bytes
    limit = min(VMEM_CAP_BYTES, need + (24 << 20))
    return pltpu.CompilerParams(dimension_semantics=semantics, vmem_limit_bytes=limit)


def _nbytes(shape, dtype):
    n = 1
    for s in shape:
        n *= s
    return n * jnp.dtype(dtype).itemsize


def _rope_tables(seq, half, segments):
    cos = jnp.ones((seq, LANES), F32)
    sin_up = jnp.zeros((seq, LANES), F32)
    sin_dn = jnp.zeros((seq, LANES), F32)
    for first, pos, theta in segments:
        inv = theta ** (-jnp.arange(half, dtype=F32) / half)
        ang = pos[:, None] * inv[None, :]
        c, s = jnp.cos(ang), jnp.sin(ang)
        cos = cos.at[:, first:first + half].set(c).at[:, first + half:first + 2 * half].set(c)
        sin_up = sin_up.at[:, first:first + half].set(-s)
        sin_dn = sin_dn.at[:, first + half:first + 2 * half].set(s)
    return cos, sin_up, sin_dn


def _rope(x, cos, sin_up, sin_dn, half):
    return (x * cos + pltpu.roll(x, LANES - half, 1) * sin_up
            + pltpu.roll(x, half, 1) * sin_dn)


def _proj_kernel(x_ref, g_ref, w_ref, b_ref, o_ref, xn_ref, *, n_sigmoid_tiles):
    j = pl.program_id(1)

    @pl.when(j == 0)
    def _():
        x = x_ref[...]
        ms = jnp.mean(x * x, axis=-1, keepdims=True)
        xn_ref[...] = (x * lax.rsqrt(ms + EPS) * g_ref[...]).astype(BF16)

    y = jnp.dot(xn_ref[...], w_ref[...], preferred_element_type=F32) + b_ref[...]

    @pl.when(j < n_sigmoid_tiles)
    def _():
        o_ref[...] = (1.0 / (1.0 + jnp.exp(-y))).astype(o_ref.dtype)

    @pl.when(j >= n_sigmoid_tiles)
    def _():
        o_ref[...] = y.astype(o_ref.dtype)


def _proj(x2d, gain, w, bias, n_sigmoid_cols, tm, tn):
    m, d = x2d.shape
    n = w.shape[1]
    kern = functools.partial(_proj_kernel, n_sigmoid_tiles=n_sigmoid_cols // tn)
    return pl.pallas_call(
        kern,
        grid=(m // tm, n // tn),
        in_specs=[
            pl.BlockSpec((tm, d), lambda i, j: (i, 0)),
            pl.BlockSpec((1, d), lambda i, j: (0, 0)),
            pl.BlockSpec((d, tn), lambda i, j: (0, j)),
            pl.BlockSpec((1, tn), lambda i, j: (0, j)),
        ],
        out_specs=pl.BlockSpec((tm, tn), lambda i, j: (i, j)),
        out_shape=jax.ShapeDtypeStruct((m, n), BF16),
        scratch_shapes=[pltpu.VMEM((tm, d), BF16)],
        compiler_params=_params(
            ("parallel", "arbitrary"),
            _nbytes((tm, d), F32), _nbytes((d, tn), BF16), _nbytes((tm, tn), BF16),
            scratch_bytes=_nbytes((tm, d), BF16) + _nbytes((tm, d), F32)),
        name="proj",
    )(x2d, gain, w, bias)


def _headnorm_rope_kernel(x_ref, g_ref, cos_ref, su_ref, sd_ref, o_ref, *, nheads, half):
    cos, su, sd = cos_ref[...], su_ref[...], sd_ref[...]
    for h in range(nheads):
        cols = slice(h * HEAD_DIM, (h + 1) * HEAD_DIM)
        x = x_ref[:, cols].astype(F32)
        ms = jnp.mean(x * x, axis=-1, keepdims=True)
        y = x * lax.rsqrt(ms + EPS) * g_ref[:, cols]
        o_ref[:, cols] = _rope(y, cos, su, sd, half).astype(o_ref.dtype)


def _headnorm_rope(p, col_off, width, nchunks, gains, tables, half, seq, tm):
    m = p.shape[0]
    assert col_off % width == 0
    cb = col_off // width
    sb = seq // tm
    kern = functools.partial(_headnorm_rope_kernel, nheads=width // HEAD_DIM, half=half)
    tab_spec = pl.BlockSpec((tm, LANES), lambda i, j: (i % sb, 0))
    return pl.pallas_call(
        kern,
        grid=(m // tm, nchunks),
        in_specs=[
            pl.BlockSpec((tm, width), lambda i, j: (i, cb + j)),
            pl.BlockSpec((None, 1, width), lambda i, j: (j, 0, 0)),
            tab_spec, tab_spec, tab_spec,
        ],
        out_specs=pl.BlockSpec((tm, width), lambda i, j: (i, j)),
        out_shape=jax.ShapeDtypeStruct((m, nchunks * width), BF16),
        compiler_params=_params(
            ("parallel", "parallel"),
            2 * _nbytes((tm, width), BF16), 3 * _nbytes((tm, LANES), F32),
            scratch_bytes=4 * _nbytes((tm, LANES), F32)),
        name="headnorm_rope",
    )(p, gains, *tables)


def _mla_prep_kernel(cq_ref, ckv_ref, kpe_ref, wq_ref, wk_ref, wv_ref, gql_ref, gkl_ref,
                     gqh_ref, gkn_ref, gkr_ref, cos_ref, su_ref, sd_ref,
                     q_ref, k_ref, v_ref):
    cos, su, sd = cos_ref[...], su_ref[...], sd_ref[...]
    half = MLA_ROPE // 2

    cq = cq_ref[...].astype(F32)
    cqn = cq * lax.rsqrt(jnp.mean(cq * cq, axis=-1, keepdims=True) + EPS) * gql_ref[...]
    q = jnp.dot(cqn.astype(BF16), wq_ref[...], preferred_element_type=F32)
    gqh = gqh_ref[...]
    for h in range(MLA_HEADS):
        qh = q[:, h * MLA_QK_PAD:(h + 1) * MLA_QK_PAD]
        ms = jnp.sum(qh * qh, axis=-1, keepdims=True) * (1.0 / MLA_QK)
        qn = qh * lax.rsqrt(ms + EPS) * gqh
        q_ref[:, h * MLA_QK_PAD:h * MLA_QK_PAD + LANES] = qn[:, :LANES].astype(q_ref.dtype)
        q_ref[:, h * MLA_QK_PAD + LANES:(h + 1) * MLA_QK_PAD] = _rope(
            qn[:, LANES:], cos, su, sd, half).astype(q_ref.dtype)

    ckv = ckv_ref[...].astype(F32)
    ckvn = (ckv * lax.rsqrt(jnp.mean(ckv * ckv, axis=-1, keepdims=True) + EPS)
            * gkl_ref[...]).astype(BF16)
    kn = jnp.dot(ckvn, wk_ref[...], preferred_element_type=F32)
    v_ref[...] = jnp.dot(ckvn, wv_ref[...], preferred_element_type=F32).astype(v_ref.dtype)
    kpe = kpe_ref[...].astype(F32)
    pe_sq = jnp.sum(kpe * kpe, axis=-1, keepdims=True)
    gkn, gkr = gkn_ref[...], gkr_ref[...]
    for h in range(MLA_HEADS):
        kh = kn[:, h * MLA_NOPE:(h + 1) * MLA_NOPE]
        ms = (jnp.sum(kh * kh, axis=-1, keepdims=True) + pe_sq) * (1.0 / MLA_QK)
        inv = lax.rsqrt(ms + EPS)
        k_ref[:, h * MLA_QK_PAD:h * MLA_QK_PAD + LANES] = (kh * inv * gkn).astype(k_ref.dtype)
        k_ref[:, h * MLA_QK_PAD + LANES:(h + 1) * MLA_QK_PAD] = _rope(
            kpe * inv * gkr, cos, su, sd, half).astype(k_ref.dtype)


def _mla_prep(p, off_cq, off_ckv, off_kpe, wq, wk, wv, gql, gkl, gqh, gkn, gkr, tables, seq, tm):
    m = p.shape[0]
    sb = seq // tm
    const = lambda shape: pl.BlockSpec(shape, lambda i: (0, 0))
    tab_spec = pl.BlockSpec((tm, LANES), lambda i: (i % sb, 0))
    qk_cols = MLA_HEADS * MLA_QK_PAD
    v_cols = MLA_HEADS * MLA_V
    return pl.pallas_call(
        _mla_prep_kernel,
        grid=(m // tm,),
        in_specs=[
            pl.BlockSpec((tm, MLA_Q_RANK), lambda i: (i, off_cq // MLA_Q_RANK)),
            pl.BlockSpec((tm, MLA_KV_RANK), lambda i: (i, off_ckv // MLA_KV_RANK)),
            pl.BlockSpec((tm, LANES), lambda i: (i, off_kpe // LANES)),
            const(wq.shape), const(wk.shape), const(wv.shape),
            const(gql.shape), const(gkl.shape), const(gqh.shape), const(gkn.shape), const(gkr.shape),
            tab_spec, tab_spec, tab_spec,
        ],
        out_specs=[
            pl.BlockSpec((tm, qk_cols), lambda i: (i, 0)),
            pl.BlockSpec((tm, qk_cols), lambda i: (i, 0)),
            pl.BlockSpec((tm, v_cols), lambda i: (i, 0)),
        ],
        out_shape=[
            jax.ShapeDtypeStruct((m, qk_cols), BF16),
            jax.ShapeDtypeStruct((m, qk_cols), BF16),
            jax.ShapeDtypeStruct((m, v_cols), BF16),
        ],
        compiler_params=_params(
            ("parallel",),
            _nbytes((tm, MLA_Q_RANK + MLA_KV_RANK + LANES), BF16),
            _nbytes(wq.shape, BF16), _nbytes(wk.shape, BF16), _nbytes(wv.shape, BF16),
            _nbytes((tm, 2 * qk_cols + v_cols), BF16), 3 * _nbytes((tm, LANES), F32),
            scratch_bytes=3 * _nbytes((tm, qk_cols), F32)),
        name="mla_prep",
    )(p, p, p, wq, wk, wv, gql, gkl, gqh, gkn, gkr, *tables)


def _attn_kernel(q_ref, k_ref, v_ref, o_ref):
    s = lax.dot_general(q_ref[...], k_ref[...], (((1,), (1,)), ((), ())),
                        preferred_element_type=F32)
    m = jnp.max(s, axis=-1, keepdims=True)
    p = jnp.exp(s - m)
    l = jnp.sum(p, axis=-1, keepdims=True)
    o = jnp.dot(p.astype(BF16), v_ref[...], preferred_element_type=F32)
    o_ref[...] = (o * (1.0 / l)).astype(o_ref.dtype)


def _attention(q_arr, k_arr, v_arr, q_off, k_off, v_off, nheads, group, dk, dv, batch, seq, tq):
    assert q_off % dk == 0 and k_off % dk == 0 and v_off % dv == 0
    nq = seq // tq
    qb, kb, vb = q_off // dk, k_off // dk, v_off // dv
    return pl.pallas_call(
        _attn_kernel,
        grid=(batch, nheads, nq),
        in_specs=[
            pl.BlockSpec((tq, dk), lambda b, h, i: (b * nq + i, qb + h)),
            pl.BlockSpec((seq, dk), lambda b, h, i: (b, kb + h // group)),
            pl.BlockSpec((seq, dv), lambda b, h, i: (b, vb + h // group)),
        ],
        out_specs=pl.BlockSpec((tq, dv), lambda b, h, i: (b * nq + i, h)),
        out_shape=jax.ShapeDtypeStruct((batch * seq, nheads * dv), BF16),
        compiler_params=_params(
            ("parallel", "parallel", "parallel"),
            _nbytes((tq, dk), BF16), _nbytes((seq, dk), BF16), _nbytes((seq, dv), BF16),
            _nbytes((tq, dv), BF16), scratch_bytes=3 * _nbytes((tq, seq), F32)),
        name="attention",
    )(q_arr, k_arr, v_arr)


def _dilated_kernel(q_ref, k_ref, v_ref, o_ref, lse_ref, *, length, radius):
    nt = length // QBLOCK
    kw = min(length, QBLOCK + 2 * radius)

    def tile(t, hg):
        cols = slice(hg * HEAD_DIM, (hg + 1) * HEAD_DIM)
        if nt == 1:
            q0 = ks = 0
        else:
            q0 = pl.multiple_of(t * QBLOCK, QBLOCK)
            ks = pl.multiple_of(jnp.clip(q0 - radius, 0, length - kw), radius)
        q = q_ref[pl.ds(q0, QBLOCK), cols]
        k = k_ref[pl.ds(ks, kw), cols]
        v = v_ref[pl.ds(ks, kw), cols]
        s = lax.dot_general(q, k, (((1,), (1,)), ((), ())), preferred_element_type=F32)
        lq = q0 + lax.broadcasted_iota(jnp.int32, (QBLOCK, kw), 0)
        lk = ks + lax.broadcasted_iota(jnp.int32, (QBLOCK, kw), 1)
        s = jnp.where(jnp.abs(lq - lk) <= radius, s, NEG_INF)
        m = jnp.max(s, axis=-1, keepdims=True)
        p = jnp.exp(s - m)
        l = jnp.sum(p, axis=-1, keepdims=True)
        o = jnp.dot(p.astype(BF16), v, preferred_element_type=F32)
        o_ref[pl.ds(q0, QBLOCK), cols] = o * (1.0 / l)
        lse_ref[pl.ds(q0, QBLOCK), cols] = jnp.broadcast_to(m + jnp.log(l), (QBLOCK, HEAD_DIM))

    for hg in range(DIL_HPG):
        if nt == 1:
            tile(0, hg)
        else:
            def body(t, carry, hg=hg):
                tile(t, hg)
                return carry
            lax.fori_loop(0, nt, body, 0)


def _dilated_group(qk_arr, p_arr, off_v, group, dilation, radius, batch, seq):
    length = seq // dilation
    width = DIL_HPG * HEAD_DIM
    qk_cols, p_cols = qk_arr.shape[1], p_arr.shape[1]
    assert off_v % width == 0 and p_cols % width == 0
    qk_view = qk_arr.reshape(batch * length, dilation * qk_cols)
    p_view = p_arr.reshape(batch * length, dilation * p_cols)
    qpr, ppr = qk_cols // width, p_cols // width
    nk = DIL_HEADS * HEAD_DIM // width
    vb = off_v // width
    kern = functools.partial(_dilated_kernel, length=length, radius=radius)
    out_sds = jax.ShapeDtypeStruct((batch * length, dilation * width), F32)
    blk = lambda f: pl.BlockSpec((length, width), f)
    o, lse = pl.pallas_call(
        kern,
        grid=(batch, dilation),
        in_specs=[
            blk(lambda b, r: (b, r * qpr + group)),
            blk(lambda b, r: (b, r * qpr + nk + group)),
            blk(lambda b, r: (b, r * ppr + vb + group)),
        ],
        out_specs=[blk(lambda b, r: (b, r)), blk(lambda b, r: (b, r))],
        out_shape=[out_sds, out_sds],
        compiler_params=_params(
            ("parallel", "parallel"),
            3 * _nbytes((length, width), BF16), 2 * _nbytes((length, width), F32)),
        name="dilated",
    )(qk_view, qk_view, p_view)
    return o.reshape(batch * seq, width), lse.reshape(batch * seq, width)


def _dil_combine_kernel(o0, l0, o1, l1, o2, l2, out_ref):
    a, b, c = l0[...], l1[...], l2[...]
    m = jnp.maximum(jnp.maximum(a, b), c)
    ea, eb, ec = jnp.exp(a - m), jnp.exp(b - m), jnp.exp(c - m)
    num = ea * o0[...] + eb * o1[...] + ec * o2[...]
    out_ref[...] = (num * (1.0 / (ea + eb + ec))).astype(out_ref.dtype)


def _dil_combine(parts, tm):
    m, width = parts[0].shape
    spec = pl.BlockSpec((tm, width), lambda i: (i, 0))
    return pl.pallas_call(
        _dil_combine_kernel,
        grid=(m // tm,),
        in_specs=[spec] * 6,
        out_specs=spec,
        out_shape=jax.ShapeDtypeStruct((m, width), BF16),
        compiler_params=_params(("parallel",), 7 * _nbytes((tm, width), F32)),
        name="dil_combine",
    )(*parts)


def _merge_kernel(oa_ref, ob_ref, oc_ref, ga_ref, gb_ref, gc_ref, wa_ref, wb_ref, wc_ref, o_ref):
    ya = jnp.dot(oa_ref[...], wa_ref[...], preferred_element_type=F32)
    yb = jnp.dot(ob_ref[...], wb_ref[...], preferred_element_type=F32)
    yc = jnp.dot(oc_ref[...], wc_ref[...], preferred_element_type=F32)
    merged = (ga_ref[...].astype(F32) * ya + gb_ref[...].astype(F32) * yb
              + gc_ref[...].astype(F32) * yc)
    o_ref[...] = merged.astype(o_ref.dtype)


def _merge(oa, ob, oc, p, w_oa, w_ob, w_oc, tm, tn):
    m = oa.shape[0]
    d = w_oa.shape[1]
    nj = d // tn
    row = lambda a: pl.BlockSpec((tm, a.shape[1]), lambda i, j: (i, 0))
    wcol = lambda w: pl.BlockSpec((w.shape[0], tn), lambda i, j: (0, j))
    gate = lambda g: pl.BlockSpec((tm, tn), lambda i, j: (i, g * nj + j))
    return pl.pallas_call(
        _merge_kernel,
        grid=(m // tm, nj),
        in_specs=[row(oa), row(ob), row(oc), gate(0), gate(1), gate(2),
                  wcol(w_oa), wcol(w_ob), wcol(w_oc)],
        out_specs=pl.BlockSpec((tm, tn), lambda i, j: (i, j)),
        out_shape=jax.ShapeDtypeStruct((m, d), BF16),
        compiler_params=_params(
            ("parallel", "parallel"),
            _nbytes((tm, oa.shape[1] + ob.shape[1] + oc.shape[1]), BF16),
            4 * _nbytes((tm, tn), BF16),
            _nbytes((w_oa.shape[0] + w_ob.shape[0] + w_oc.shape[0], tn), BF16),
            scratch_bytes=4 * _nbytes((tm, tn), F32)),
        name="merge",
    )(oa, ob, oc, p, p, p, w_oa, w_ob, w_oc)


def _resid_matmul_kernel(x_ref, a_ref, w_ref, o_ref):
    o_ref[...] = x_ref[...] + jnp.dot(a_ref[...], w_ref[...], preferred_element_type=F32)


def _resid_matmul(x2d, a, w, tm, tn):
    m, d = x2d.shape
    k = a.shape[1]
    return pl.pallas_call(
        _resid_matmul_kernel,
        grid=(m // tm, d // tn),
        in_specs=[
            pl.BlockSpec((tm, tn), lambda i, j: (i, j)),
            pl.BlockSpec((tm, k), lambda i, j: (i, 0)),
            pl.BlockSpec((k, tn), lambda i, j: (0, j)),
        ],
        out_specs=pl.BlockSpec((tm, tn), lambda i, j: (i, j)),
        out_shape=jax.ShapeDtypeStruct((m, d), F32),
        compiler_params=_params(
            ("parallel", "parallel"),
            2 * _nbytes((tm, tn), F32), _nbytes((tm, k), BF16), _nbytes((k, tn), BF16)),
        name="out_proj",
    )(x2d, a, w)


def _mlp_kernel(x_ref, g_ref, wu_ref, wd_ref, o_ref, xn_ref):
    c = pl.program_id(1)

    @pl.when(c == 0)
    def _():
        x = x_ref[...]
        ms = jnp.mean(x * x, axis=-1, keepdims=True)
        xn_ref[...] = (x * lax.rsqrt(ms + EPS) * g_ref[...]).astype(BF16)
        o_ref[...] = x

    h = jnp.dot(xn_ref[...], wu_ref[...], preferred_element_type=F32)
    h = jnp.square(jnp.maximum(h, 0.0)).astype(BF16)
    o_ref[...] += jnp.dot(h, wd_ref[...], preferred_element_type=F32)


def _mlp(x2d, gain, w_up, w_down, tm, tf):
    m, d = x2d.shape
    f = w_up.shape[1]
    return pl.pallas_call(
        _mlp_kernel,
        grid=(m // tm, f // tf),
        in_specs=[
            pl.BlockSpec((tm, d), lambda i, c: (i, 0)),
            pl.BlockSpec((1, d), lambda i, c: (0, 0)),
            pl.BlockSpec((d, tf), lambda i, c: (0, c)),
            pl.BlockSpec((tf, d), lambda i, c: (c, 0)),
        ],
        out_specs=pl.BlockSpec((tm, d), lambda i, c: (i, 0)),
        out_shape=jax.ShapeDtypeStruct((m, d), F32),
        scratch_shapes=[pltpu.VMEM((tm, d), BF16)],
        compiler_params=_params(
            ("parallel", "arbitrary"),
            2 * _nbytes((tm, d), F32), 2 * _nbytes((d, tf), BF16),
            scratch_bytes=_nbytes((tm, d), BF16) + _nbytes((tm, tf), F32)),
        name="mlp",
    )(x2d, gain, w_up, w_down)


def _largest_tile(n, cap):
    t = min(n, cap)
    while n % t:
        t //= 2
    return t


def kernel(x, attn_norm, w_in, b_gate, mla_q_lat_norm, w_uq, mla_kv_lat_norm, w_ukv,
           mla_q_head_norm, mla_k_head_norm, gqa_q_norm, gqa_k_norm, dil_q_norm, dil_k_norm,
           w_oa, w_ob, w_oc, w_out, mlp_norm, w_up, w_down):
    batch, seq, d = x.shape
    depth = w_in.shape[0]
    m = batch * seq
    d_ff = w_up.shape[2]
    gate_cols = 3 * d
    assert w_in.shape[2] == A_COLS + B_COLS + C_COLS + gate_cols
    assert seq % GRID_W == 0 and seq % (16 * QBLOCK) == 0 and d % LANES == 0
    tn_d = _largest_tile(d, 1024)

    tn_proj = 1024
    gate_pad = _round_up(gate_cols, tn_proj)
    off_bq = gate_pad
    off_bk = off_bq + GQA_HEADS * HEAD_DIM
    off_bv = off_bk + GQA_KV_HEADS * HEAD_DIM
    off_cq = off_bv + GQA_KV_HEADS * HEAD_DIM
    off_cv = off_cq + 2 * DIL_HEADS * HEAD_DIM
    off_aq = off_cv + DIL_HEADS * HEAD_DIM
    off_akv = off_aq + MLA_Q_RANK
    off_ape = off_akv + MLA_KV_RANK
    n_proj = _round_up(off_ape + LANES, tn_proj)

    tm_big = _largest_tile(m, 1024)
    tm_seq = _largest_tile(seq, 1024)
    tm_mla = _largest_tile(seq, 512)
    tq = _largest_tile(seq, 512)

    pos = jnp.arange(seq, dtype=F32)
    row = jnp.repeat(jnp.arange(seq // GRID_W, dtype=F32), GRID_W)
    col = jnp.tile(jnp.arange(GRID_W, dtype=F32), seq // GRID_W)
    tab_mla = _rope_tables(seq, MLA_ROPE // 2, [(0, pos, MLA_ROPE_THETA)])
    tab_gqa = _rope_tables(seq, HEAD_DIM // 4,
                           [(0, row, AXIAL_THETA), (HEAD_DIM // 2, col, AXIAL_THETA)])
    tab_dil = _rope_tables(seq, PARTIAL_ROPE_DIM // 2, [(0, pos, PARTIAL_ROPE_THETA)])

    x2d = x.reshape(m, d)
    for l in range(depth):
        wl = w_in[l]
        w_p = jnp.concatenate([
            wl[:, A_COLS + B_COLS + C_COLS:],
            jnp.zeros((d, gate_pad - gate_cols), F32),
            wl[:, A_COLS:A_COLS + B_COLS + C_COLS],
            wl[:, :A_COLS],
            jnp.zeros((d, n_proj - off_ape - MLA_ROPE), F32),
        ], axis=1).astype(BF16)
        bias = jnp.concatenate([b_gate[l], jnp.zeros((n_proj - gate_cols,), F32)])[None, :]

        p = _proj(x2d, attn_norm[l][None, :], w_p, bias, gate_pad, tm_big, tn_proj)

        wq = jnp.pad(w_uq[l].reshape(MLA_Q_RANK, MLA_HEADS, MLA_QK),
                     ((0, 0), (0, 0), (0, MLA_QK_PAD - MLA_QK))
                     ).reshape(MLA_Q_RANK, MLA_HEADS * MLA_QK_PAD).astype(BF16)
        wkv = w_ukv[l].reshape(MLA_KV_RANK, MLA_HEADS, MLA_NOPE + MLA_V)
        wk = wkv[:, :, :MLA_NOPE].reshape(MLA_KV_RANK, MLA_HEADS * MLA_NOPE).astype(BF16)
        wv = wkv[:, :, MLA_NOPE:].reshape(MLA_KV_RANK, MLA_HEADS * MLA_V).astype(BF16)
        gqh = jnp.pad(mla_q_head_norm[l] * (MLA_QK ** -0.5), (0, MLA_QK_PAD - MLA_QK))[None, :]
        gkn = mla_k_head_norm[l][None, :MLA_NOPE]
        gkr = jnp.pad(mla_k_head_norm[l][MLA_NOPE:], (0, LANES - MLA_ROPE))[None, :]
        qa, ka, va = _mla_prep(p, off_aq, off_akv, off_ape, wq, wk, wv,
                               mla_q_lat_norm[l][None, :], mla_kv_lat_norm[l][None, :],
                               gqh, gkn, gkr, tab_mla, seq, tm_mla)
        oa = _attention(qa, ka, va, 0, 0, 0, MLA_HEADS, 1, MLA_QK_PAD, MLA_V, batch, seq, tq)

        gq = jnp.tile(gqa_q_norm[l] * (HEAD_DIM ** -0.5), GQA_HEADS)[None, None, :]
        gk = jnp.tile(gqa_k_norm[l], GQA_KV_HEADS)[None, None, :]
        qb = _headnorm_rope(p, off_bq, GQA_HEADS * HEAD_DIM, 1, gq, tab_gqa, HEAD_DIM // 4, seq, tm_seq)
        kb = _headnorm_rope(p, off_bk, GQA_KV_HEADS * HEAD_DIM, 1, gk, tab_gqa, HEAD_DIM // 4, seq, tm_seq)
        ob = _attention(qb, kb, p, 0, 0, off_bv, GQA_HEADS, GQA_GROUP, HEAD_DIM, HEAD_DIM,
                        batch, seq, tq)

        width = DIL_HPG * HEAD_DIM
        gdq = jnp.tile(dil_q_norm[l] * (HEAD_DIM ** -0.5), DIL_HPG)
        gdk = jnp.tile(dil_k_norm[l], DIL_HPG)
        nchunk = DIL_HEADS // DIL_HPG
        gd = jnp.stack([gdq] * nchunk + [gdk] * nchunk)[:, None, :]
        qkc = _headnorm_rope(p, off_cq, width, 2 * nchunk, gd, tab_dil, PARTIAL_ROPE_DIM // 2,
                             seq, tm_seq)
        parts = []
        for gi, (window, dilation) in enumerate(DIL_PATTERNS):
            o_g, lse_g = _dilated_group(qkc, p, off_cv, gi, dilation, window // (2 * dilation),
                                        batch, seq)
            parts += [o_g, lse_g]
        oc = _dil_combine(parts, tm_big)

        merged = _merge(oa, ob, oc, p, w_oa[l].astype(BF16), w_ob[l].astype(BF16),
                        w_oc[l].astype(BF16), tm_big, tn_d)
        x2d = _resid_matmul(x2d, merged, w_out[l].astype(BF16), tm_big, tn_d)
        x2d = _mlp(x2d, mlp_norm[l][None, :], w_up[l].astype(BF16), w_down[l].astype(BF16),
                   _largest_tile(m, 512), _largest_tile(d_ff, 1024))
    return x2d.reshape(batch, seq, d)
```

```python
import functools

import jax
import jax.numpy as jnp
from jax import lax
from jax.experimental import pallas as pl
from jax.experimental.pallas import tpu as pltpu

F32 = jnp.float32
BF16 = jnp.bfloat16

LANES = 128
VMEM_CAP_BYTES = 60000 * 1024

HEAD_DIM = 128
GRID_W = 64
EPS = 1e-6
NEG_INF = -1e30

MLA_HEADS = 8
MLA_Q_RANK = 512
MLA_KV_RANK = 256
MLA_NOPE = 128
MLA_ROPE = 64
MLA_V = 128
MLA_ROPE_THETA = 10000.0
MLA_QK = MLA_NOPE + MLA_ROPE
MLA_QK_PAD = 2 * LANES

GQA_HEADS = 8
GQA_KV_HEADS = 2
GQA_GROUP = GQA_HEADS // GQA_KV_HEADS
AXIAL_THETA = 10000.0

DIL_PATTERNS = ((128, 1), (512, 4), (2048, 16))
DIL_HPG = 4
DIL_HEADS = DIL_HPG * len(DIL_PATTERNS)
PARTIAL_ROPE_DIM = HEAD_DIM // 4
PARTIAL_ROPE_THETA = 500000.0
QBLOCK = 128
ATTN_SUB_ROWS = 256

A_COLS = MLA_Q_RANK + MLA_KV_RANK + MLA_ROPE
B_COLS = (GQA_HEADS + 2 * GQA_KV_HEADS) * HEAD_DIM
C_COLS = 3 * DIL_HEADS * HEAD_DIM


def _round_up(n, m):
    return -(-n // m) * m


def _params(semantics, *block_bytes, scratch_bytes=0):
    need = 2 * sum(block_bytes) + scratch_bytes
    limit = min(VMEM_CAP_BYTES, need + (24 << 20))
    return pltpu.CompilerParams(dimension_semantics=semantics, vmem_limit_bytes=limit)


def _nbytes(shape, dtype):
    n = 1
    for s in shape:
        n *= s
    return n * jnp.dtype(dtype).itemsize


def _rope_tables(seq, half, segments):
    cos = jnp.ones((seq, LANES), F32)
    sin_up = jnp.zeros((seq, LANES), F32)
    sin_dn = jnp.zeros((seq, LANES), F32)
    for first, pos, theta in segments:
        inv = theta ** (-jnp.arange(half, dtype=F32) / half)
        ang = pos[:, None] * inv[None, :]
        c, s = jnp.cos(ang), jnp.sin(ang)
        cos = cos.at[:, first:first + half].set(c).at[:, first + half:first + 2 * half].set(c)
        sin_up = sin_up.at[:, first:first + half].set(-s)
        sin_dn = sin_dn.at[:, first + half:first + 2 * half].set(s)
    return cos, sin_up, sin_dn


def _rope(x, cos, sin_up, sin_dn, half):
    return (x * cos + pltpu.roll(x, LANES - half, 1) * sin_up
            + pltpu.roll(x, half, 1) * sin_dn)


def _proj_kernel(x_ref, g_ref, w_ref, b_ref, o_ref, xn_ref, *, gate):
    @pl.when(pl.program_id(1) == 0)
    def _():
        x = x_ref[...]
        ms = jnp.mean(x * x, axis=-1, keepdims=True)
        xn_ref[...] = (x * lax.rsqrt(ms + EPS) * g_ref[...]).astype(BF16)

    y = jnp.dot(xn_ref[...], w_ref[...], preferred_element_type=F32) + b_ref[...]
    if gate:
        y = 1.0 / (1.0 + jnp.exp(-y))
    o_ref[...] = y.astype(o_ref.dtype)


def _proj(x2d, gain, w, bias, gate, tm, tn):
    m, d = x2d.shape
    n = w.shape[1]
    kern = functools.partial(_proj_kernel, gate=gate)
    return pl.pallas_call(
        kern,
        grid=(m // tm, n // tn),
        in_specs=[
            pl.BlockSpec((tm, d), lambda i, j: (i, 0)),
            pl.BlockSpec((1, d), lambda i, j: (0, 0)),
            pl.BlockSpec((d, tn), lambda i, j: (0, j)),
            pl.BlockSpec((1, tn), lambda i, j: (0, j)),
        ],
        out_specs=pl.BlockSpec((tm, tn), lambda i, j: (i, j)),
        out_shape=jax.ShapeDtypeStruct((m, n), BF16),
        scratch_shapes=[pltpu.VMEM((tm, d), BF16)],
        compiler_params=_params(
            ("parallel", "arbitrary"),
            _nbytes((tm, d), F32), _nbytes((d, tn), BF16), _nbytes((tm, tn), BF16),
            scratch_bytes=_nbytes((tm, d), BF16) + _nbytes((tm, d), F32)),
        name="proj",
    )(x2d, gain, w, bias)


def _headnorm_rope_kernel(x_ref, g_ref, cos_ref, su_ref, sd_ref, o_ref, *, nheads, half):
    cos, su, sd = cos_ref[...], su_ref[...], sd_ref[...]
    for h in range(nheads):
        cols = slice(h * HEAD_DIM, (h + 1) * HEAD_DIM)
        x = x_ref[:, cols].astype(F32)
        ms = jnp.mean(x * x, axis=-1, keepdims=True)
        y = x * lax.rsqrt(ms + EPS) * g_ref[:, cols]
        o_ref[:, cols] = _rope(y, cos, su, sd, half).astype(o_ref.dtype)


def _headnorm_rope(p, col_off, width, nchunks, gains, tables, half, seq, tm):
    m = p.shape[0]
    assert col_off % width == 0
    cb = col_off // width
    sb = seq // tm
    kern = functools.partial(_headnorm_rope_kernel, nheads=width // HEAD_DIM, half=half)
    tab_spec = pl.BlockSpec((tm, LANES), lambda i, j: (i % sb, 0))
    return pl.pallas_call(
        kern,
        grid=(m // tm, nchunks),
        in_specs=[
            pl.BlockSpec((tm, width), lambda i, j: (i, cb + j)),
            pl.BlockSpec((None, 1, width), lambda i, j: (j, 0, 0)),
            tab_spec, tab_spec, tab_spec,
        ],
        out_specs=pl.BlockSpec((tm, width), lambda i, j: (i, j)),
        out_shape=jax.ShapeDtypeStruct((m, nchunks * width), BF16),
        compiler_params=_params(
            ("parallel", "parallel"),
            2 * _nbytes((tm, width), BF16), 3 * _nbytes((tm, LANES), F32),
            scratch_bytes=4 * _nbytes((tm, LANES), F32)),
        name="headnorm_rope",
    )(p, gains, *tables)


def _mla_prep_kernel(cq_ref, ckv_ref, kpe_ref, wq_ref, wk_ref, wv_ref, gql_ref, gkl_ref,
                     gqh_ref, gkn_ref, gkr_ref, cos_ref, su_ref, sd_ref,
                     q_ref, k_ref, v_ref):
    cos, su, sd = cos_ref[...], su_ref[...], sd_ref[...]
    half = MLA_ROPE // 2

    cq = cq_ref[...].astype(F32)
    cqn = cq * lax.rsqrt(jnp.mean(cq * cq, axis=-1, keepdims=True) + EPS) * gql_ref[...]
    q = jnp.dot(cqn.astype(BF16), wq_ref[...], preferred_element_type=F32)
    gqh = gqh_ref[...]
    for h in range(MLA_HEADS):
        qh = q[:, h * MLA_QK_PAD:(h + 1) * MLA_QK_PAD]
        ms = jnp.sum(qh * qh, axis=-1, keepdims=True) * (1.0 / MLA_QK)
        qn = qh * lax.rsqrt(ms + EPS) * gqh
        q_ref[:, h * MLA_QK_PAD:h * MLA_QK_PAD + LANES] = qn[:, :LANES].astype(q_ref.dtype)
        q_ref[:, h * MLA_QK_PAD + LANES:(h + 1) * MLA_QK_PAD] = _rope(
            qn[:, LANES:], cos, su, sd, half).astype(q_ref.dtype)

    ckv = ckv_ref[...].astype(F32)
    ckvn = (ckv * lax.rsqrt(jnp.mean(ckv * ckv, axis=-1, keepdims=True) + EPS)
            * gkl_ref[...]).astype(BF16)
    kn = jnp.dot(ckvn, wk_ref[...], preferred_element_type=F32)
    v_ref[...] = jnp.dot(ckvn, wv_ref[...], preferred_element_type=F32).astype(v_ref.dtype)
    kpe = kpe_ref[...].astype(F32)
    pe_sq = jnp.sum(kpe * kpe, axis=-1, keepdims=True)
    gkn, gkr = gkn_ref[...], gkr_ref[...]
    for h in range(MLA_HEADS):
        kh = kn[:, h * MLA_NOPE:(h + 1) * MLA_NOPE]
        ms = (jnp.sum(kh * kh, axis=-1, keepdims=True) + pe_sq) * (1.0 / MLA_QK)
        inv = lax.rsqrt(ms + EPS)
        k_ref[:, h * MLA_QK_PAD:h * MLA_QK_PAD + LANES] = (kh * inv * gkn).astype(k_ref.dtype)
        k_ref[:, h * MLA_QK_PAD + LANES:(h + 1) * MLA_QK_PAD] = _rope(
            kpe * inv * gkr, cos, su, sd, half).astype(k_ref.dtype)


def _mla_prep(p, off_cq, off_ckv, off_kpe, wq, wk, wv, gql, gkl, gqh, gkn, gkr, tables, seq, tm):
    m = p.shape[0]
    sb = seq // tm
    const = lambda shape: pl.BlockSpec(shape, lambda i: (0, 0))
    tab_spec = pl.BlockSpec((tm, LANES), lambda i: (i % sb, 0))
    qk_cols = MLA_HEADS * MLA_QK_PAD
    v_cols = MLA_HEADS * MLA_V
    return pl.pallas_call(
        _mla_prep_kernel,
        grid=(m // tm,),
        in_specs=[
            pl.BlockSpec((tm, MLA_Q_RANK), lambda i: (i, off_cq // MLA_Q_RANK)),
            pl.BlockSpec((tm, MLA_KV_RANK), lambda i: (i, off_ckv // MLA_KV_RANK)),
            pl.BlockSpec((tm, LANES), lambda i: (i, off_kpe // LANES)),
            const(wq.shape), const(wk.shape), const(wv.shape),
            const(gql.shape), const(gkl.shape), const(gqh.shape), const(gkn.shape), const(gkr.shape),
            tab_spec, tab_spec, tab_spec,
        ],
        out_specs=[
            pl.BlockSpec((tm, qk_cols), lambda i: (i, 0)),
            pl.BlockSpec((tm, qk_cols), lambda i: (i, 0)),
            pl.BlockSpec((tm, v_cols), lambda i: (i, 0)),
        ],
        out_shape=[
            jax.ShapeDtypeStruct((m, qk_cols), BF16),
            jax.ShapeDtypeStruct((m, qk_cols), BF16),
            jax.ShapeDtypeStruct((m, v_cols), BF16),
        ],
        compiler_params=_params(
            ("parallel",),
            _nbytes((tm, MLA_Q_RANK + MLA_KV_RANK + LANES), BF16),
            _nbytes(wq.shape, BF16), _nbytes(wk.shape, BF16), _nbytes(wv.shape, BF16),
            _nbytes((tm, 2 * qk_cols + v_cols), BF16), 3 * _nbytes((tm, LANES), F32),
            scratch_bytes=3 * _nbytes((tm, qk_cols), F32)),
        name="mla_prep",
    )(p, p, p, wq, wk, wv, gql, gkl, gqh, gkn, gkr, *tables)


def _attn_kernel(q_ref, k_ref, v_ref, o_ref, *, sub):
    for r0 in range(0, q_ref.shape[0], sub):
        rows = slice(r0, r0 + sub)
        s = lax.dot_general(q_ref[rows, :], k_ref[...], (((1,), (1,)), ((), ())),
                            preferred_element_type=F32)
        m = jnp.max(s, axis=-1, keepdims=True)
        p = jnp.exp(s - m)
        l = jnp.sum(p, axis=-1, keepdims=True)
        o = jnp.dot(p.astype(BF16), v_ref[...], preferred_element_type=F32)
        o_ref[rows, :] = (o * (1.0 / l)).astype(o_ref.dtype)


def _attention(q_arr, k_arr, v_arr, q_off, k_off, v_off, nheads, group, dk, dv, batch, seq, tq):
    assert q_off % dk == 0 and k_off % dk == 0 and v_off % dv == 0
    nq = seq // tq
    qb, kb, vb = q_off // dk, k_off // dk, v_off // dv
    return pl.pallas_call(
        functools.partial(_attn_kernel, sub=min(tq, ATTN_SUB_ROWS)),
        grid=(batch, nheads, nq),
        in_specs=[
            pl.BlockSpec((tq, dk), lambda b, h, i: (b * nq + i, qb + h)),
            pl.BlockSpec((seq, dk), lambda b, h, i: (b, kb + h // group)),
            pl.BlockSpec((seq, dv), lambda b, h, i: (b, vb + h // group)),
        ],
        out_specs=pl.BlockSpec((tq, dv), lambda b, h, i: (b * nq + i, h)),
        out_shape=jax.ShapeDtypeStruct((batch * seq, nheads * dv), BF16),
        compiler_params=_params(
            ("parallel", "parallel", "parallel"),
            _nbytes((tq, dk), BF16), _nbytes((seq, dk), BF16), _nbytes((seq, dv), BF16),
            _nbytes((tq, dv), BF16), scratch_bytes=3 * _nbytes((tq, seq), F32)),
        name="attention",
    )(q_arr, k_arr, v_arr)


def _band_tile(q, k, v, delta, radius):
    s = lax.dot_general(q, k, (((1,), (1,)), ((), ())), preferred_element_type=F32)
    rel = (lax.broadcasted_iota(jnp.int32, s.shape, 0)
           - lax.broadcasted_iota(jnp.int32, s.shape, 1)) + delta
    s = jnp.where(jnp.abs(rel) <= radius, s, NEG_INF)
    m = jnp.max(s, axis=-1, keepdims=True)
    p = jnp.exp(s - m)
    l = jnp.sum(p, axis=-1, keepdims=True)
    o = jnp.dot(p.astype(BF16), v, preferred_element_type=F32) * (1.0 / l)
    return o, jnp.broadcast_to(m + jnp.log(l), o.shape)


def _dilated_kernel(q0_ref, q1_ref, q2_ref, k0_ref, k1_ref, k2_ref, v0_ref, v1_ref, v2_ref,
                    gq_ref, gk_ref, cos_ref, su_ref, sd_ref, out_ref,
                    qf_ref, kf_ref, vf_ref, of_ref, lf_ref, *, seq, patterns):
    half = PARTIAL_ROPE_DIM // 2
    q_refs, k_refs, v_refs = (q0_ref, q1_ref, q2_ref), (k0_ref, k1_ref, k2_ref), (v0_ref, v1_ref, v2_ref)

    def prep(x_ref, g_ref):
        x = x_ref[...].astype(F32)
        ms = jnp.mean(x * x, axis=-1, keepdims=True)
        return _rope(x * lax.rsqrt(ms + EPS) * g_ref[...], cos_ref[...], su_ref[...], sd_ref[...], half)

    for g, (window, dil) in enumerate(patterns):
        radius = window // (2 * dil)
        length = seq // dil
        kw = min(length, QBLOCK + 2 * radius)
        qf_ref[...] = prep(q_refs[g], gq_ref)
        kf_ref[...] = prep(k_refs[g], gk_ref)
        vf_ref[...] = v_refs[g][...].astype(F32)
        for r in range(dil):
            def rows(start, n, r=r, dil=dil):
                return pl.ds(start, n) if dil == 1 else pl.ds(start * dil + r, n, stride=dil)
            for t in range(length // QBLOCK):
                q0 = t * QBLOCK
                ks = min(max(q0 - radius, 0), length - kw)
                o, lse = _band_tile(qf_ref[rows(q0, QBLOCK), :].astype(BF16),
                                    kf_ref[rows(ks, kw), :].astype(BF16),
                                    vf_ref[rows(ks, kw), :].astype(BF16), q0 - ks, radius)
                of_ref[g, rows(q0, QBLOCK), :] = o
                lf_ref[g, rows(q0, QBLOCK), :] = lse

    la, lb, lc = lf_ref[0], lf_ref[1], lf_ref[2]
    m = jnp.maximum(jnp.maximum(la, lb), lc)
    ea, eb, ec = jnp.exp(la - m), jnp.exp(lb - m), jnp.exp(lc - m)
    num = ea * of_ref[0] + eb * of_ref[1] + ec * of_ref[2]
    out_ref[...] = (num * (1.0 / (ea + eb + ec))).astype(out_ref.dtype)


def _dilated(p, off_q, off_k, off_v, gq, gk, tables, batch, seq):
    ng = len(DIL_PATTERNS)
    assert off_q % HEAD_DIM == 0 and off_k % HEAD_DIM == 0 and off_v % HEAD_DIM == 0
    kern = functools.partial(_dilated_kernel, seq=seq, patterns=DIL_PATTERNS)

    def head(off, g):
        cb = off // HEAD_DIM + g * DIL_HPG
        return pl.BlockSpec((seq, HEAD_DIM), lambda b, h: (b, cb + h))

    const = lambda shape: pl.BlockSpec(shape, lambda b, h: (0, 0))
    slab = _nbytes((seq, HEAD_DIM), F32)
    return pl.pallas_call(
        kern,
        grid=(batch, DIL_HPG),
        in_specs=([head(off_q, g) for g in range(ng)] + [head(off_k, g) for g in range(ng)]
                  + [head(off_v, g) for g in range(ng)]
                  + [const((1, HEAD_DIM))] * 2 + [const((seq, LANES))] * 3),
        out_specs=pl.BlockSpec((seq, HEAD_DIM), lambda b, h: (b, h)),
        out_shape=jax.ShapeDtypeStruct((batch * seq, DIL_HPG * HEAD_DIM), BF16),
        scratch_shapes=[pltpu.VMEM((seq, HEAD_DIM), F32)] * 3
                       + [pltpu.VMEM((ng, seq, HEAD_DIM), F32)] * 2,
        compiler_params=_params(
            ("parallel", "parallel"),
            (3 * ng + 1) * _nbytes((seq, HEAD_DIM), BF16), 3 * slab,
            scratch_bytes=(3 + 2 * ng) * slab + 4 * slab),
        name="dilated",
    )(*([p] * (3 * ng)), gq, gk, *tables)


def _merge_kernel(oa_ref, ob_ref, oc_ref, ga_ref, gb_ref, gc_ref, wa_ref, wb_ref, wc_ref, o_ref):
    ya = jnp.dot(oa_ref[...], wa_ref[...], preferred_element_type=F32)
    yb = jnp.dot(ob_ref[...], wb_ref[...], preferred_element_type=F32)
    yc = jnp.dot(oc_ref[...], wc_ref[...], preferred_element_type=F32)
    merged = (ga_ref[...].astype(F32) * ya + gb_ref[...].astype(F32) * yb
              + gc_ref[...].astype(F32) * yc)
    o_ref[...] = merged.astype(o_ref.dtype)


def _merge(oa, ob, oc, p, w_oa, w_ob, w_oc, tm, tn):
    m = oa.shape[0]
    d = w_oa.shape[1]
    nj = d // tn
    row = lambda a: pl.BlockSpec((tm, a.shape[1]), lambda i, j: (i, 0))
    wcol = lambda w: pl.BlockSpec((w.shape[0], tn), lambda i, j: (0, j))
    gate = lambda g: pl.BlockSpec((tm, tn), lambda i, j: (i, g * nj + j))
    return pl.pallas_call(
        _merge_kernel,
        grid=(m // tm, nj),
        in_specs=[row(oa), row(ob), row(oc), gate(0), gate(1), gate(2),
                  wcol(w_oa), wcol(w_ob), wcol(w_oc)],
        out_specs=pl.BlockSpec((tm, tn), lambda i, j: (i, j)),
        out_shape=jax.ShapeDtypeStruct((m, d), BF16),
        compiler_params=_params(
            ("parallel", "parallel"),
            _nbytes((tm, oa.shape[1] + ob.shape[1] + oc.shape[1]), BF16),
            4 * _nbytes((tm, tn), BF16),
            _nbytes((w_oa.shape[0] + w_ob.shape[0] + w_oc.shape[0], tn), BF16),
            scratch_bytes=4 * _nbytes((tm, tn), F32)),
        name="merge",
    )(oa, ob, oc, p, p, p, w_oa, w_ob, w_oc)


def _resid_matmul_kernel(x_ref, a_ref, w_ref, o_ref):
    o_ref[...] = x_ref[...] + jnp.dot(a_ref[...], w_ref[...], preferred_element_type=F32)


def _resid_matmul(x2d, a, w, tm, tn):
    m, d = x2d.shape
    k = a.shape[1]
    return pl.pallas_call(
        _resid_matmul_kernel,
        grid=(m // tm, d // tn),
        in_specs=[
            pl.BlockSpec((tm, tn), lambda i, j: (i, j)),
            pl.BlockSpec((tm, k), lambda i, j: (i, 0)),
            pl.BlockSpec((k, tn), lambda i, j: (0, j)),
        ],
        out_specs=pl.BlockSpec((tm, tn), lambda i, j: (i, j)),
        out_shape=jax.ShapeDtypeStruct((m, d), F32),
        compiler_params=_params(
            ("parallel", "parallel"),
            2 * _nbytes((tm, tn), F32), _nbytes((tm, k), BF16), _nbytes((k, tn), BF16)),
        name="out_proj",
    )(x2d, a, w)


def _mlp_kernel(x_ref, g_ref, wu_ref, wd_ref, o_ref, xn_ref):
    c = pl.program_id(1)

    @pl.when(c == 0)
    def _():
        x = x_ref[...]
        ms = jnp.mean(x * x, axis=-1, keepdims=True)
        xn_ref[...] = (x * lax.rsqrt(ms + EPS) * g_ref[...]).astype(BF16)
        o_ref[...] = x

    h = jnp.dot(xn_ref[...], wu_ref[...], preferred_element_type=F32)
    h = jnp.square(jnp.maximum(h, 0.0)).astype(BF16)
    o_ref[...] += jnp.dot(h, wd_ref[...], preferred_element_type=F32)


def _mlp(x2d, gain, w_up, w_down, tm, tf):
    m, d = x2d.shape
    f = w_up.shape[1]
    return pl.pallas_call(
        _mlp_kernel,
        grid=(m // tm, f // tf),
        in_specs=[
            pl.BlockSpec((tm, d), lambda i, c: (i, 0)),
            pl.BlockSpec((1, d), lambda i, c: (0, 0)),
            pl.BlockSpec((d, tf), lambda i, c: (0, c)),
            pl.BlockSpec((tf, d), lambda i, c: (c, 0)),
        ],
        out_specs=pl.BlockSpec((tm, d), lambda i, c: (i, 0)),
        out_shape=jax.ShapeDtypeStruct((m, d), F32),
        scratch_shapes=[pltpu.VMEM((tm, d), BF16)],
        compiler_params=_params(
            ("parallel", "arbitrary"),
            2 * _nbytes((tm, d), F32), 2 * _nbytes((d, tf), BF16),
            scratch_bytes=_nbytes((tm, d), BF16) + _nbytes((tm, tf), F32)),
        name="mlp",
    )(x2d, gain, w_up, w_down)


def _largest_tile(n, cap):
    t = min(n, cap)
    while n % t:
        t //= 2
    return t


def kernel(x, attn_norm, w_in, b_gate, mla_q_lat_norm, w_uq, mla_kv_lat_norm, w_ukv,
           mla_q_head_norm, mla_k_head_norm, gqa_q_norm, gqa_k_norm, dil_q_norm, dil_k_norm,
           w_oa, w_ob, w_oc, w_out, mlp_norm, w_up, w_down):
    batch, seq, d = x.shape
    depth = w_in.shape[0]
    m = batch * seq
    d_ff = w_up.shape[2]
    gate_cols = 3 * d
    assert w_in.shape[2] == A_COLS + B_COLS + C_COLS + gate_cols
    assert seq % GRID_W == 0 and seq % (16 * QBLOCK) == 0 and d % LANES == 0
    tn_d = _largest_tile(d, 1024)

    tn_proj = 1024
    off_bq = 0
    off_bk = off_bq + GQA_HEADS * HEAD_DIM
    off_bv = off_bk + GQA_KV_HEADS * HEAD_DIM
    off_cq = off_bv + GQA_KV_HEADS * HEAD_DIM
    off_ck = off_cq + DIL_HEADS * HEAD_DIM
    off_cv = off_ck + DIL_HEADS * HEAD_DIM
    off_aq = off_cv + DIL_HEADS * HEAD_DIM
    off_akv = off_aq + MLA_Q_RANK
    off_ape = off_akv + MLA_KV_RANK
    n_proj = _round_up(off_ape + LANES, tn_proj)

    tm_big = _largest_tile(m, 1024)
    tm_seq = _largest_tile(seq, 1024)
    tm_mla = _largest_tile(seq, 512)
    tq = _largest_tile(seq, 512)

    pos = jnp.arange(seq, dtype=F32)
    row = jnp.repeat(jnp.arange(seq // GRID_W, dtype=F32), GRID_W)
    col = jnp.tile(jnp.arange(GRID_W, dtype=F32), seq // GRID_W)
    tab_mla = _rope_tables(seq, MLA_ROPE // 2, [(0, pos, MLA_ROPE_THETA)])
    tab_gqa = _rope_tables(seq, HEAD_DIM // 4,
                           [(0, row, AXIAL_THETA), (HEAD_DIM // 2, col, AXIAL_THETA)])
    tab_dil = _rope_tables(seq, PARTIAL_ROPE_DIM // 2, [(0, pos, PARTIAL_ROPE_THETA)])

    x2d = x.reshape(m, d)
    for l in range(depth):
        wl = w_in[l]
        w_p = jnp.concatenate([
            wl[:, A_COLS:A_COLS + B_COLS + C_COLS],
            wl[:, :A_COLS],
            jnp.zeros((d, n_proj - off_ape - MLA_ROPE), F32),
        ], axis=1).astype(BF16)
        w_g = wl[:, A_COLS + B_COLS + C_COLS:].astype(BF16)
        gain = attn_norm[l][None, :]
        p = _proj(x2d, gain, w_p, jnp.zeros((1, n_proj), F32), False, tm_big, tn_proj)
        gates = _proj(x2d, gain, w_g, b_gate[l][None, :], True, tm_big,
                      _largest_tile(gate_cols, 1024))

        wq = jnp.pad(w_uq[l].reshape(MLA_Q_RANK, MLA_HEADS, MLA_QK),
                     ((0, 0), (0, 0), (0, MLA_QK_PAD - MLA_QK))
                     ).reshape(MLA_Q_RANK, MLA_HEADS * MLA_QK_PAD).astype(BF16)
        wkv = w_ukv[l].reshape(MLA_KV_RANK, MLA_HEADS, MLA_NOPE + MLA_V)
        wk = wkv[:, :, :MLA_NOPE].reshape(MLA_KV_RANK, MLA_HEADS * MLA_NOPE).astype(BF16)
        wv = wkv[:, :, MLA_NOPE:].reshape(MLA_KV_RANK, MLA_HEADS * MLA_V).astype(BF16)
        gqh = jnp.pad(mla_q_head_norm[l] * (MLA_QK ** -0.5), (0, MLA_QK_PAD - MLA_QK))[None, :]
        gkn = mla_k_head_norm[l][None, :MLA_NOPE]
        gkr = jnp.pad(mla_k_head_norm[l][MLA_NOPE:], (0, LANES - MLA_ROPE))[None, :]
        qa, ka, va = _mla_prep(p, off_aq, off_akv, off_ape, wq, wk, wv,
                               mla_q_lat_norm[l][None, :], mla_kv_lat_norm[l][None, :],
                               gqh, gkn, gkr, tab_mla, seq, tm_mla)
        oa = _attention(qa, ka, va, 0, 0, 0, MLA_HEADS, 1, MLA_QK_PAD, MLA_V, batch, seq, tq)

        gq = jnp.tile(gqa_q_norm[l] * (HEAD_DIM ** -0.5), GQA_HEADS)[None, None, :]
        gk = jnp.tile(gqa_k_norm[l], GQA_KV_HEADS)[None, None, :]
        qb = _headnorm_rope(p, off_bq, GQA_HEADS * HEAD_DIM, 1, gq, tab_gqa, HEAD_DIM // 4, seq, tm_seq)
        kb = _headnorm_rope(p, off_bk, GQA_KV_HEADS * HEAD_DIM, 1, gk, tab_gqa, HEAD_DIM // 4, seq, tm_seq)
        ob = _attention(qb, kb, p, 0, 0, off_bv, GQA_HEADS, GQA_GROUP, HEAD_DIM, HEAD_DIM,
                        batch, seq, tq)

        oc = _dilated(p, off_cq, off_ck, off_cv,
                      (dil_q_norm[l] * (HEAD_DIM ** -0.5))[None, :], dil_k_norm[l][None, :],
                      tab_dil, batch, seq)

        merged = _merge(oa, ob, oc, gates, w_oa[l].astype(BF16), w_ob[l].astype(BF16),
                        w_oc[l].astype(BF16), tm_big, tn_d)
        x2d = _resid_matmul(x2d, merged, w_out[l].astype(BF16), tm_big, tn_d)
        x2d = _mlp(x2d, mlp_norm[l][None, :], w_up[l].astype(BF16), w_down[l].astype(BF16),
                   _largest_tile(m, 1024), _largest_tile(d_ff, 512))
    return x2d.reshape(batch, seq, d)
```

```python
import functools

import jax
import jax.numpy as jnp
from jax import lax
from jax.experimental import pallas as pl
from jax.experimental.pallas import tpu as pltpu

F32 = jnp.float32
BF16 = jnp.bfloat16

LANES = 128
VMEM_CAP_BYTES = 60000 * 1024

HEAD_DIM = 128
GRID_W = 64
EPS = 1e-6
NEG_INF = -1e30

MLA_HEADS = 8
MLA_Q_RANK = 512
MLA_KV_RANK = 256
MLA_NOPE = 128
MLA_ROPE = 64
MLA_V = 128
MLA_ROPE_THETA = 10000.0
MLA_QK = MLA_NOPE + MLA_ROPE
MLA_QK_PAD = 2 * LANES

GQA_HEADS = 8
GQA_KV_HEADS = 2
GQA_GROUP = GQA_HEADS // GQA_KV_HEADS
AXIAL_THETA = 10000.0

DIL_PATTERNS = ((128, 1), (512, 4), (2048, 16))
DIL_HPG = 4
DIL_HEADS = DIL_HPG * len(DIL_PATTERNS)
PARTIAL_ROPE_DIM = HEAD_DIM // 4
PARTIAL_ROPE_THETA = 500000.0
QBLOCK = 128
ATTN_SUB_ROWS = 256
DIL_PREP_ROWS = 256

A_COLS = MLA_Q_RANK + MLA_KV_RANK + MLA_ROPE
B_COLS = (GQA_HEADS + 2 * GQA_KV_HEADS) * HEAD_DIM
C_COLS = 3 * DIL_HEADS * HEAD_DIM


def _round_up(n, m):
    return -(-n // m) * m


def _params(semantics, *block_bytes, scratch_bytes=0):
    need = 2 * sum(block_bytes) + scratch_bytes
    limit = min(VMEM_CAP_BYTES, need + (24 << 20))
    return pltpu.CompilerParams(dimension_semantics=semantics, vmem_limit_bytes=limit)


def _nbytes(shape, dtype):
    n = 1
    for s in shape:
        n *= s
    return n * jnp.dtype(dtype).itemsize


ROPE_PAIR_SHIFT = LANES // 2


def _rope_tables(seq, segments):
    cos = jnp.ones((seq, LANES), F32)
    sin = jnp.zeros((seq, LANES), F32)
    for first, half, pos, theta in segments:
        inv = theta ** (-jnp.arange(half, dtype=F32) / half)
        ang = pos[:, None] * inv[None, :]
        c, s = jnp.cos(ang), jnp.sin(ang)
        hi = first + ROPE_PAIR_SHIFT
        cos = cos.at[:, first:first + half].set(c).at[:, hi:hi + half].set(c)
        sin = sin.at[:, first:first + half].set(-s).at[:, hi:hi + half].set(s)
    return cos, sin


def _rope(x, cos, sin):
    return x * cos + pltpu.roll(x, ROPE_PAIR_SHIFT, 1) * sin


def _regroup_head_dims(w, pieces):
    lead = w.shape[:-1]
    wh = w.reshape(lead + (w.shape[-1] // HEAD_DIM, HEAD_DIM))
    return jnp.concatenate([wh[..., a:b] for a, b in pieces], axis=-1).reshape(w.shape)


GQA_PIECES = ((0, 32), (64, 96), (32, 64), (96, 128))
DIL_PIECES = ((0, 16), (32, 80), (16, 32), (80, 128))


def _proj_kernel(x_ref, g_ref, w_ref, b_ref, o_ref, xn_ref, *, gate):
    @pl.when(pl.program_id(1) == 0)
    def _():
        x = x_ref[...]
        ms = jnp.mean(x * x, axis=-1, keepdims=True)
        xn_ref[...] = (x * lax.rsqrt(ms + EPS) * g_ref[...]).astype(BF16)

    y = jnp.dot(xn_ref[...], w_ref[...], preferred_element_type=F32) + b_ref[...]
    if gate:
        y = 1.0 / (1.0 + jnp.exp(-y))
    o_ref[...] = y.astype(o_ref.dtype)


def _proj(x2d, gain, w, bias, gate, tm, tn):
    m, d = x2d.shape
    n = w.shape[1]
    kern = functools.partial(_proj_kernel, gate=gate)
    return pl.pallas_call(
        kern,
        grid=(m // tm, n // tn),
        in_specs=[
            pl.BlockSpec((tm, d), lambda i, j: (i, 0)),
            pl.BlockSpec((1, d), lambda i, j: (0, 0)),
            pl.BlockSpec((d, tn), lambda i, j: (0, j)),
            pl.BlockSpec((1, tn), lambda i, j: (0, j)),
        ],
        out_specs=pl.BlockSpec((tm, tn), lambda i, j: (i, j)),
        out_shape=jax.ShapeDtypeStruct((m, n), BF16),
        scratch_shapes=[pltpu.VMEM((tm, d), BF16)],
        compiler_params=_params(
            ("parallel", "arbitrary"),
            _nbytes((tm, d), F32), _nbytes((d, tn), BF16), _nbytes((tm, tn), BF16),
            scratch_bytes=_nbytes((tm, d), BF16) + _nbytes((tm, d), F32)),
        name="proj",
    )(x2d, gain, w, bias)


def _row_sumsq(x):
    sq = x * x
    hi = sq.astype(BF16)
    lo = (sq - hi.astype(F32)).astype(BF16)
    ones = jnp.ones((x.shape[-1], LANES), BF16)
    return (jnp.dot(hi, ones, preferred_element_type=F32)
            + jnp.dot(lo, ones, preferred_element_type=F32))


def _headnorm_rope_kernel(x_ref, g_ref, cos_ref, sin_ref, o_ref, *, nheads):
    cos, sin = cos_ref[...], sin_ref[...]
    for h in range(nheads):
        cols = slice(h * HEAD_DIM, (h + 1) * HEAD_DIM)
        x = x_ref[:, cols].astype(F32)
        ms = _row_sumsq(x) * (1.0 / HEAD_DIM)
        y = x * lax.rsqrt(ms + EPS) * g_ref[:, cols]
        o_ref[:, cols] = _rope(y, cos, sin).astype(o_ref.dtype)


def _headnorm_rope(p, col_off, width, gains, tables, seq, tm):
    m = p.shape[0]
    assert col_off % width == 0
    cb = col_off // width
    sb = seq // tm
    kern = functools.partial(_headnorm_rope_kernel, nheads=width // HEAD_DIM)
    tab_spec = pl.BlockSpec((tm, LANES), lambda i: (i % sb, 0))
    return pl.pallas_call(
        kern,
        grid=(m // tm,),
        in_specs=[
            pl.BlockSpec((tm, width), lambda i: (i, cb)),
            pl.BlockSpec((1, width), lambda i: (0, 0)),
            tab_spec, tab_spec,
        ],
        out_specs=pl.BlockSpec((tm, width), lambda i: (i, 0)),
        out_shape=jax.ShapeDtypeStruct((m, width), BF16),
        compiler_params=_params(
            ("parallel",),
            2 * _nbytes((tm, width), BF16), 2 * _nbytes((tm, LANES), F32),
            scratch_bytes=4 * _nbytes((tm, LANES), F32)),
        name="headnorm_rope",
    )(p, gains, *tables)


def _mla_prep_kernel(cq_ref, ckv_ref, kpe_ref, wq_ref, wk_ref, wv_ref, gql_ref, gkl_ref,
                     gqh_ref, gkn_ref, gkr_ref, cos_ref, sin_ref,
                     q_ref, k_ref, v_ref):
    cos, sin = cos_ref[...], sin_ref[...]

    cq = cq_ref[...].astype(F32)
    cqn = cq * lax.rsqrt(jnp.mean(cq * cq, axis=-1, keepdims=True) + EPS) * gql_ref[...]
    q = jnp.dot(cqn.astype(BF16), wq_ref[...], preferred_element_type=F32)
    gqh = gqh_ref[...]
    for h in range(MLA_HEADS):
        qh = q[:, h * MLA_QK_PAD:(h + 1) * MLA_QK_PAD]
        ms = jnp.sum(qh * qh, axis=-1, keepdims=True) * (1.0 / MLA_QK)
        qn = qh * lax.rsqrt(ms + EPS) * gqh
        q_ref[:, h * MLA_QK_PAD:h * MLA_QK_PAD + LANES] = qn[:, :LANES].astype(q_ref.dtype)
        q_ref[:, h * MLA_QK_PAD + LANES:(h + 1) * MLA_QK_PAD] = _rope(
            qn[:, LANES:], cos, sin).astype(q_ref.dtype)

    ckv = ckv_ref[...].astype(F32)
    ckvn = (ckv * lax.rsqrt(jnp.mean(ckv * ckv, axis=-1, keepdims=True) + EPS)
            * gkl_ref[...]).astype(BF16)
    kn = jnp.dot(ckvn, wk_ref[...], preferred_element_type=F32)
    v_ref[...] = jnp.dot(ckvn, wv_ref[...], preferred_element_type=F32).astype(v_ref.dtype)
    kpe = kpe_ref[...].astype(F32)
    pe_sq = jnp.sum(kpe * kpe, axis=-1, keepdims=True)
    gkn, gkr = gkn_ref[...], gkr_ref[...]
    for h in range(MLA_HEADS):
        kh = kn[:, h * MLA_NOPE:(h + 1) * MLA_NOPE]
        ms = (jnp.sum(kh * kh, axis=-1, keepdims=True) + pe_sq) * (1.0 / MLA_QK)
        inv = lax.rsqrt(ms + EPS)
        k_ref[:, h * MLA_QK_PAD:h * MLA_QK_PAD + LANES] = (kh * inv * gkn).astype(k_ref.dtype)
        k_ref[:, h * MLA_QK_PAD + LANES:(h + 1) * MLA_QK_PAD] = _rope(
            kpe * inv * gkr, cos, sin).astype(k_ref.dtype)


def _mla_prep(p, off_cq, off_ckv, off_kpe, wq, wk, wv, gql, gkl, gqh, gkn, gkr, tables, seq, tm):
    m = p.shape[0]
    sb = seq // tm
    const = lambda shape: pl.BlockSpec(shape, lambda i: (0, 0))
    tab_spec = pl.BlockSpec((tm, LANES), lambda i: (i % sb, 0))
    qk_cols = MLA_HEADS * MLA_QK_PAD
    v_cols = MLA_HEADS * MLA_V
    return pl.pallas_call(
        _mla_prep_kernel,
        grid=(m // tm,),
        in_specs=[
            pl.BlockSpec((tm, MLA_Q_RANK), lambda i: (i, off_cq // MLA_Q_RANK)),
            pl.BlockSpec((tm, MLA_KV_RANK), lambda i: (i, off_ckv // MLA_KV_RANK)),
            pl.BlockSpec((tm, LANES), lambda i: (i, off_kpe // LANES)),
            const(wq.shape), const(wk.shape), const(wv.shape),
            const(gql.shape), const(gkl.shape), const(gqh.shape), const(gkn.shape), const(gkr.shape),
            tab_spec, tab_spec,
        ],
        out_specs=[
            pl.BlockSpec((tm, qk_cols), lambda i: (i, 0)),
            pl.BlockSpec((tm, qk_cols), lambda i: (i, 0)),
            pl.BlockSpec((tm, v_cols), lambda i: (i, 0)),
        ],
        out_shape=[
            jax.ShapeDtypeStruct((m, qk_cols), BF16),
            jax.ShapeDtypeStruct((m, qk_cols), BF16),
            jax.ShapeDtypeStruct((m, v_cols), BF16),
        ],
        compiler_params=_params(
            ("parallel",),
            _nbytes((tm, MLA_Q_RANK + MLA_KV_RANK + LANES), BF16),
            _nbytes(wq.shape, BF16), _nbytes(wk.shape, BF16), _nbytes(wv.shape, BF16),
            _nbytes((tm, 2 * qk_cols + v_cols), BF16), 2 * _nbytes((tm, LANES), F32),
            scratch_bytes=3 * _nbytes((tm, qk_cols), F32)),
        name="mla_prep",
    )(p, p, p, wq, wk, wv, gql, gkl, gqh, gkn, gkr, *tables)


def _attn_kernel(q_ref, k_ref, v_ref, o_ref, *, sub):
    for r0 in range(0, q_ref.shape[0], sub):
        rows = slice(r0, r0 + sub)
        s = lax.dot_general(q_ref[rows, :], k_ref[...], (((1,), (1,)), ((), ())),
                            preferred_element_type=F32)
        m = jnp.max(s, axis=-1, keepdims=True)
        p = jnp.exp(s - m)
        l = jnp.sum(p, axis=-1, keepdims=True)
        o = jnp.dot(p.astype(BF16), v_ref[...], preferred_element_type=F32)
        o_ref[rows, :] = (o * (1.0 / l)).astype(o_ref.dtype)


def _attention(q_arr, k_arr, v_arr, q_off, k_off, v_off, nheads, group, dk, dv, batch, seq, tq):
    assert q_off % dk == 0 and k_off % dk == 0 and v_off % dv == 0
    nq = seq // tq
    qb, kb, vb = q_off // dk, k_off // dk, v_off // dv
    sub = min(tq, ATTN_SUB_ROWS)
    return pl.pallas_call(
        functools.partial(_attn_kernel, sub=sub),
        grid=(batch, nheads, nq),
        in_specs=[
            pl.BlockSpec((tq, dk), lambda b, h, i: (b * nq + i, qb + h)),
            pl.BlockSpec((seq, dk), lambda b, h, i: (b, kb + h // group)),
            pl.BlockSpec((seq, dv), lambda b, h, i: (b, vb + h // group)),
        ],
        out_specs=pl.BlockSpec((tq, dv), lambda b, h, i: (b * nq + i, h)),
        out_shape=jax.ShapeDtypeStruct((batch * seq, nheads * dv), BF16),
        compiler_params=_params(
            ("parallel", "parallel", "parallel"),
            _nbytes((tq, dk), BF16), _nbytes((seq, dk), BF16), _nbytes((seq, dv), BF16),
            _nbytes((tq, dv), BF16), scratch_bytes=4 * _nbytes((sub, seq), F32)),
        name="attention",
    )(q_arr, k_arr, v_arr)


def _band_tile(q, k, v, delta, radius):
    s = lax.dot_general(q, k, (((1,), (1,)), ((), ())), preferred_element_type=F32)
    rel = (lax.broadcasted_iota(jnp.int32, s.shape, 0)
           - lax.broadcasted_iota(jnp.int32, s.shape, 1)) + delta
    s = jnp.where(jnp.abs(rel) <= radius, s, NEG_INF)
    m = jnp.max(s, axis=-1, keepdims=True)
    p = jnp.exp(s - m)
    l = jnp.sum(p, axis=-1, keepdims=True)
    o = jnp.dot(p.astype(BF16), v, preferred_element_type=F32) * (1.0 / l)
    return o, jnp.broadcast_to(m + jnp.log(l), o.shape)


def _dilated_kernel(q0_ref, q1_ref, q2_ref, k0_ref, k1_ref, k2_ref, v0_ref, v1_ref, v2_ref,
                    gq_ref, gk_ref, cos_ref, sin_ref, out_ref,
                    qb_ref, kb_ref, qf_ref, kf_ref, vf_ref, of_ref, lf_ref, *, seq, patterns):
    q_refs, k_refs, v_refs = (q0_ref, q1_ref, q2_ref), (k0_ref, k1_ref, k2_ref), (v0_ref, v1_ref, v2_ref)
    strided = [g for g, (_, dil) in enumerate(patterns) if dil > 1]
    dense = [g for g, (_, dil) in enumerate(patterns) if dil == 1]

    def prepared(src, gain, rows):
        x = src[rows, :].astype(F32)
        ms = _row_sumsq(x) * (1.0 / HEAD_DIM)
        return _rope(x * lax.rsqrt(ms + EPS) * gain[...], cos_ref[rows, :], sin_ref[rows, :])

    for g, (window, dil) in enumerate(patterns):
        radius = window // (2 * dil)
        length = seq // dil
        kw = min(length, QBLOCK + 2 * radius)
        slot = strided.index(g) if dil > 1 else dense.index(g)
        for r0 in range(0, seq, DIL_PREP_ROWS):
            chunk = slice(r0, r0 + DIL_PREP_ROWS)
            if dil > 1:
                qf_ref[slot, chunk, :] = prepared(q_refs[g], gq_ref, chunk)
                kf_ref[slot, chunk, :] = prepared(k_refs[g], gk_ref, chunk)
                vf_ref[slot, chunk, :] = v_refs[g][chunk, :].astype(F32)
            else:
                qb_ref[slot, chunk, :] = prepared(q_refs[g], gq_ref, chunk).astype(BF16)
                kb_ref[slot, chunk, :] = prepared(k_refs[g], gk_ref, chunk).astype(BF16)
        for r in range(dil):
            def rows(start, n, r=r, dil=dil):
                return pl.ds(start, n) if dil == 1 else pl.ds(start * dil + r, n, stride=dil)

            def window_of(dense_ref, copy, start, n, dil=dil, slot=slot, rows=rows):
                if dil == 1:
                    return dense_ref[rows(start, n), :]
                return copy[slot, rows(start, n), :].astype(BF16)

            for t in range(length // QBLOCK):
                q0 = t * QBLOCK
                ks = min(max(q0 - radius, 0), length - kw)
                o, lse = _band_tile(
                    window_of(qb_ref.at[slot] if dil == 1 else None, qf_ref, q0, QBLOCK),
                    window_of(kb_ref.at[slot] if dil == 1 else None, kf_ref, ks, kw),
                    window_of(v_refs[g], vf_ref, ks, kw), q0 - ks, radius)
                of_ref[g, rows(q0, QBLOCK), :] = o
                lf_ref[g, rows(q0, QBLOCK), :] = lse

    la, lb, lc = lf_ref[0], lf_ref[1], lf_ref[2]
    m = jnp.maximum(jnp.maximum(la, lb), lc)
    ea, eb, ec = jnp.exp(la - m), jnp.exp(lb - m), jnp.exp(lc - m)
    num = ea * of_ref[0] + eb * of_ref[1] + ec * of_ref[2]
    out_ref[...] = (num * (1.0 / (ea + eb + ec))).astype(out_ref.dtype)


def _dilated(p, off_q, off_k, off_v, gq, gk, tables, batch, seq):
    ng = len(DIL_PATTERNS)
    n_strided = sum(dil > 1 for _, dil in DIL_PATTERNS)
    kern = functools.partial(_dilated_kernel, seq=seq, patterns=DIL_PATTERNS)
    group_w = DIL_HPG * HEAD_DIM

    def head(off, g):
        assert off % HEAD_DIM == 0
        cb = (off + g * group_w) // HEAD_DIM
        return pl.BlockSpec((seq, HEAD_DIM), lambda b, h: (b, cb + h))

    const = lambda shape: pl.BlockSpec(shape, lambda b, h: (0, 0))
    slab = _nbytes((seq, HEAD_DIM), F32)
    return pl.pallas_call(
        kern,
        grid=(batch, DIL_HPG),
        in_specs=([head(off_q, g) for g in range(ng)] + [head(off_k, g) for g in range(ng)]
                  + [head(off_v, g) for g in range(ng)]
                  + [const((1, HEAD_DIM))] * 2 + [const((seq, LANES))] * 2),
        out_specs=pl.BlockSpec((seq, HEAD_DIM), lambda b, h: (b, h)),
        out_shape=jax.ShapeDtypeStruct((batch * seq, group_w), BF16),
        scratch_shapes=[pltpu.VMEM((ng - n_strided, seq, HEAD_DIM), BF16)] * 2
                       + [pltpu.VMEM((n_strided, seq, HEAD_DIM), F32)] * 3
                       + [pltpu.VMEM((ng, seq, HEAD_DIM), F32)] * 2,
        compiler_params=_params(
            ("parallel", "parallel"),
            (3 * ng + 1) * _nbytes((seq, HEAD_DIM), BF16), 2 * slab,
            scratch_bytes=(ng - n_strided + 3 * n_strided + 2 * ng) * slab + 4 * slab),
        name="dilated",
    )(*([p] * (3 * ng)), gq, gk, *tables)


def _merge_kernel(oa_ref, ob_ref, oc_ref, ga_ref, gb_ref, gc_ref, wa_ref, wb_ref, wc_ref, o_ref):
    ya = jnp.dot(oa_ref[...], wa_ref[...], preferred_element_type=F32)
    yb = jnp.dot(ob_ref[...], wb_ref[...], preferred_element_type=F32)
    yc = jnp.dot(oc_ref[...], wc_ref[...], preferred_element_type=F32)
    merged = (ga_ref[...].astype(F32) * ya + gb_ref[...].astype(F32) * yb
              + gc_ref[...].astype(F32) * yc)
    o_ref[...] = merged.astype(o_ref.dtype)


def _merge(oa, ob, oc, p, w_oa, w_ob, w_oc, tm, tn):
    m = oa.shape[0]
    d = w_oa.shape[1]
    nj = d // tn
    row = lambda a: pl.BlockSpec((tm, a.shape[1]), lambda i, j: (i, 0))
    wcol = lambda w: pl.BlockSpec((w.shape[0], tn), lambda i, j: (0, j))
    gate = lambda g: pl.BlockSpec((tm, tn), lambda i, j: (i, g * nj + j))
    return pl.pallas_call(
        _merge_kernel,
        grid=(m // tm, nj),
        in_specs=[row(oa), row(ob), row(oc), gate(0), gate(1), gate(2),
                  wcol(w_oa), wcol(w_ob), wcol(w_oc)],
        out_specs=pl.BlockSpec((tm, tn), lambda i, j: (i, j)),
        out_shape=jax.ShapeDtypeStruct((m, d), BF16),
        compiler_params=_params(
            ("parallel", "parallel"),
            _nbytes((tm, oa.shape[1] + ob.shape[1] + oc.shape[1]), BF16),
            4 * _nbytes((tm, tn), BF16),
            _nbytes((w_oa.shape[0] + w_ob.shape[0] + w_oc.shape[0], tn), BF16),
            scratch_bytes=4 * _nbytes((tm, tn), F32)),
        name="merge",
    )(oa, ob, oc, p, p, p, w_oa, w_ob, w_oc)


def _resid_matmul_kernel(x_ref, a_ref, w_ref, o_ref):
    o_ref[...] = x_ref[...] + jnp.dot(a_ref[...], w_ref[...], preferred_element_type=F32)


def _resid_matmul(x2d, a, w, tm, tn):
    m, d = x2d.shape
    k = a.shape[1]
    return pl.pallas_call(
        _resid_matmul_kernel,
        grid=(m // tm, d // tn),
        in_specs=[
            pl.BlockSpec((tm, tn), lambda i, j: (i, j)),
            pl.BlockSpec((tm, k), lambda i, j: (i, 0)),
            pl.BlockSpec((k, tn), lambda i, j: (0, j)),
        ],
        out_specs=pl.BlockSpec((tm, tn), lambda i, j: (i, j)),
        out_shape=jax.ShapeDtypeStruct((m, d), F32),
        compiler_params=_params(
            ("parallel", "parallel"),
            2 * _nbytes((tm, tn), F32), _nbytes((tm, k), BF16), _nbytes((k, tn), BF16)),
        name="out_proj",
    )(x2d, a, w)


def _mlp_kernel(x_ref, g_ref, wu_ref, wd_ref, o_ref, xn_ref):
    c = pl.program_id(1)

    @pl.when(c == 0)
    def _():
        x = x_ref[...]
        ms = jnp.mean(x * x, axis=-1, keepdims=True)
        xn_ref[...] = (x * lax.rsqrt(ms + EPS) * g_ref[...]).astype(BF16)
        o_ref[...] = x

    h = jnp.dot(xn_ref[...], wu_ref[...], preferred_element_type=F32)
    h = jnp.square(jnp.maximum(h, 0.0)).astype(BF16)
    o_ref[...] += jnp.dot(h, wd_ref[...], preferred_element_type=F32)


def _mlp(x2d, gain, w_up, w_down, tm, tf):
    m, d = x2d.shape
    f = w_up.shape[1]
    return pl.pallas_call(
        _mlp_kernel,
        grid=(m // tm, f // tf),
        in_specs=[
            pl.BlockSpec((tm, d), lambda i, c: (i, 0)),
            pl.BlockSpec((1, d), lambda i, c: (0, 0)),
            pl.BlockSpec((d, tf), lambda i, c: (0, c)),
            pl.BlockSpec((tf, d), lambda i, c: (c, 0)),
        ],
        out_specs=pl.BlockSpec((tm, d), lambda i, c: (i, 0)),
        out_shape=jax.ShapeDtypeStruct((m, d), F32),
        scratch_shapes=[pltpu.VMEM((tm, d), BF16)],
        compiler_params=_params(
            ("parallel", "arbitrary"),
            2 * _nbytes((tm, d), F32), 2 * _nbytes((d, tf), BF16),
            scratch_bytes=_nbytes((tm, d), BF16) + _nbytes((tm, tf), F32)),
        name="mlp",
    )(x2d, gain, w_up, w_down)


def _largest_tile(n, cap):
    t = min(n, cap)
    while n % t:
        t //= 2
    return t


def kernel(x, attn_norm, w_in, b_gate, mla_q_lat_norm, w_uq, mla_kv_lat_norm, w_ukv,
           mla_q_head_norm, mla_k_head_norm, gqa_q_norm, gqa_k_norm, dil_q_norm, dil_k_norm,
           w_oa, w_ob, w_oc, w_out, mlp_norm, w_up, w_down):
    batch, seq, d = x.shape
    depth = w_in.shape[0]
    m = batch * seq
    d_ff = w_up.shape[2]
    gate_cols = 3 * d
    assert w_in.shape[2] == A_COLS + B_COLS + C_COLS + gate_cols
    assert seq % GRID_W == 0 and seq % (16 * QBLOCK) == 0 and d % LANES == 0
    tn_d = _largest_tile(d, 1024)

    tn_proj = 1024
    nq_b, nk_b = GQA_HEADS * HEAD_DIM, GQA_KV_HEADS * HEAD_DIM
    nh_c = DIL_HEADS * HEAD_DIM
    off_bq = 0
    off_bk = off_bq + nq_b
    off_bv = off_bk + nk_b
    off_cq = off_bv + nk_b
    off_ck = off_cq + nh_c
    off_cv = off_ck + nh_c
    off_aq = off_cv + nh_c
    off_akv = off_aq + MLA_Q_RANK
    off_ape = off_akv + MLA_KV_RANK
    n_proj = _round_up(off_ape + LANES, tn_proj)

    tm_big = _largest_tile(m, 1024)
    tm_seq = _largest_tile(seq, 1024)
    tm_mla = _largest_tile(seq, 512)

    pos = jnp.arange(seq, dtype=F32)
    row = jnp.repeat(jnp.arange(seq // GRID_W, dtype=F32), GRID_W)
    col = jnp.tile(jnp.arange(GRID_W, dtype=F32), seq // GRID_W)
    tab_mla = _rope_tables(seq, [(0, MLA_ROPE // 2, pos, MLA_ROPE_THETA)])
    tab_gqa = _rope_tables(seq, [(0, HEAD_DIM // 4, row, AXIAL_THETA),
                                 (HEAD_DIM // 4, HEAD_DIM // 4, col, AXIAL_THETA)])
    tab_dil = _rope_tables(seq, [(0, PARTIAL_ROPE_DIM // 2, pos, PARTIAL_ROPE_THETA)])

    def rope_slab(w):
        half = MLA_ROPE // 2
        zeros = jnp.zeros(w.shape[:-1] + (ROPE_PAIR_SHIFT - half,), w.dtype)
        return jnp.concatenate([w[..., :half], zeros, w[..., half:], zeros], axis=-1)

    x2d = x.reshape(m, d)
    for l in range(depth):
        wl = w_in[l].astype(BF16)
        b0, c0 = A_COLS, A_COLS + B_COLS
        w_p = jnp.concatenate([
            _regroup_head_dims(wl[:, b0:b0 + nq_b + nk_b], GQA_PIECES),
            wl[:, b0 + nq_b + nk_b:c0],
            _regroup_head_dims(wl[:, c0:c0 + 2 * nh_c], DIL_PIECES),
            wl[:, c0 + 2 * nh_c:c0 + C_COLS],
            wl[:, :MLA_Q_RANK + MLA_KV_RANK],
            rope_slab(wl[:, MLA_Q_RANK + MLA_KV_RANK:A_COLS]),
            jnp.zeros((d, n_proj - off_ape - LANES), BF16),
        ], axis=1)
        w_g = wl[:, c0 + C_COLS:]
        gain = attn_norm[l][None, :]
        p = _proj(x2d, gain, w_p, jnp.zeros((1, n_proj), F32), False, tm_big, tn_proj)
        gates = _proj(x2d, gain, w_g, b_gate[l][None, :], True, tm_big,
                      _largest_tile(gate_cols, 1024))

        wq = w_uq[l].astype(BF16).reshape(MLA_Q_RANK, MLA_HEADS, MLA_QK)
        wq = jnp.concatenate([wq[..., :MLA_NOPE], rope_slab(wq[..., MLA_NOPE:])], axis=-1
                             ).reshape(MLA_Q_RANK, MLA_HEADS * MLA_QK_PAD)
        wkv = w_ukv[l].astype(BF16).reshape(MLA_KV_RANK, MLA_HEADS, MLA_NOPE + MLA_V)
        wk = wkv[:, :, :MLA_NOPE].reshape(MLA_KV_RANK, MLA_HEADS * MLA_NOPE)
        wv = wkv[:, :, MLA_NOPE:].reshape(MLA_KV_RANK, MLA_HEADS * MLA_V)
        gqh = mla_q_head_norm[l] * (MLA_QK ** -0.5)
        gqh = jnp.concatenate([gqh[:MLA_NOPE], rope_slab(gqh[MLA_NOPE:])])[None, :]
        gkn = mla_k_head_norm[l][None, :MLA_NOPE]
        gkr = rope_slab(mla_k_head_norm[l][MLA_NOPE:])[None, :]
        qa, ka, va = _mla_prep(p, off_aq, off_akv, off_ape, wq, wk, wv,
                               mla_q_lat_norm[l][None, :], mla_kv_lat_norm[l][None, :],
                               gqh, gkn, gkr, tab_mla, seq, tm_mla)
        oa = _attention(qa, ka, va, 0, 0, 0, MLA_HEADS, 1, MLA_QK_PAD, MLA_V, batch, seq, seq)

        scale = HEAD_DIM ** -0.5
        gq = jnp.tile(_regroup_head_dims(gqa_q_norm[l] * scale, GQA_PIECES), GQA_HEADS)[None, :]
        gk = jnp.tile(_regroup_head_dims(gqa_k_norm[l], GQA_PIECES), GQA_KV_HEADS)[None, :]
        qb = _headnorm_rope(p, off_bq, nq_b, gq, tab_gqa, seq, tm_seq)
        kb = _headnorm_rope(p, off_bk, nk_b, gk, tab_gqa, seq, tm_seq)
        ob = _attention(qb, kb, p, 0, 0, off_bv, GQA_HEADS, GQA_GROUP, HEAD_DIM, HEAD_DIM,
                        batch, seq, seq)

        oc = _dilated(p, off_cq, off_ck, off_cv,
                      _regroup_head_dims(dil_q_norm[l] * scale, DIL_PIECES)[None, :],
                      _regroup_head_dims(dil_k_norm[l], DIL_PIECES)[None, :],
                      tab_dil, batch, seq)

        merged = _merge(oa, ob, oc, gates, w_oa[l].astype(BF16), w_ob[l].astype(BF16),
                        w_oc[l].astype(BF16), tm_big, tn_d)
        x2d = _resid_matmul(x2d, merged, w_out[l].astype(BF16), tm_big, tn_d)
        x2d = _mlp(x2d, mlp_norm[l][None, :], w_up[l].astype(BF16), w_down[l].astype(BF16),
                   _largest_tile(m, 1024), _largest_tile(d_ff, 512))
    return x2d.reshape(batch, seq, d)
```

```python
import functools

import jax
import jax.numpy as jnp
from jax import lax
from jax.experimental import pallas as pl
from jax.experimental.pallas import tpu as pltpu

F32 = jnp.float32
BF16 = jnp.bfloat16

LANES = 128
VMEM_CAP_BYTES = 60000 * 1024

HEAD_DIM = 128
GRID_W = 64
EPS = 1e-6
NEG_INF = -1e30

MLA_HEADS = 8
MLA_Q_RANK = 512
MLA_KV_RANK = 256
MLA_NOPE = 128
MLA_ROPE = 64
MLA_V = 128
MLA_ROPE_THETA = 10000.0
MLA_QK = MLA_NOPE + MLA_ROPE
MLA_QK_PAD = 2 * LANES

GQA_HEADS = 8
GQA_KV_HEADS = 2
GQA_GROUP = GQA_HEADS // GQA_KV_HEADS
AXIAL_THETA = 10000.0

DIL_PATTERNS = ((128, 1), (512, 4), (2048, 16))
DIL_HPG = 4
DIL_HEADS = DIL_HPG * len(DIL_PATTERNS)
PARTIAL_ROPE_DIM = HEAD_DIM // 4
PARTIAL_ROPE_THETA = 500000.0
QBLOCK = 128
ATTN_SUB_ROWS = 256
LOG2_E = 1.4426950408889634
ATTN_HEADS_PER_STEP = 2
DIL_PREP_ROWS = 256

A_COLS = MLA_Q_RANK + MLA_KV_RANK + MLA_ROPE
B_COLS = (GQA_HEADS + 2 * GQA_KV_HEADS) * HEAD_DIM
C_COLS = 3 * DIL_HEADS * HEAD_DIM


def _round_up(n, m):
    return -(-n // m) * m


def _params(semantics, *block_bytes, scratch_bytes=0):
    need = 2 * sum(block_bytes) + scratch_bytes
    limit = min(VMEM_CAP_BYTES, need + (24 << 20))
    return pltpu.CompilerParams(dimension_semantics=semantics, vmem_limit_bytes=limit)


def _nbytes(shape, dtype):
    n = 1
    for s in shape:
        n *= s
    return n * jnp.dtype(dtype).itemsize


ROPE_PAIR_SHIFT = LANES // 2


def _rope_tables(seq, dist, segments):
    cos = jnp.ones((seq, LANES), F32)
    sin_x1 = jnp.zeros((seq, LANES), F32)
    sin_x2 = jnp.zeros((seq, LANES), F32)
    for first, half, pos, theta in segments:
        inv = theta ** (-jnp.arange(half, dtype=F32) / half)
        ang = pos[:, None] * inv[None, :]
        c, s = jnp.cos(ang), jnp.sin(ang)
        hi = first + dist
        cos = cos.at[:, first:first + half].set(c).at[:, hi:hi + half].set(c)
        sin_x1 = sin_x1.at[:, first:first + half].set(-s)
        sin_x2 = sin_x2.at[:, hi:hi + half].set(s)
    if 2 * dist == LANES:
        return (cos, sin_x1 + sin_x2), (dist,)
    return (cos, sin_x1, sin_x2), (LANES - dist, dist)


def _rope(x, tables, shifts):
    out = x * tables[0]
    for table, shift in zip(tables[1:], shifts):
        out = out + pltpu.roll(x, shift, 1) * table
    return out


def _proj_kernel(x_ref, g_ref, w_ref, b_ref, o_ref, xn_ref, *, gate):
    @pl.when(pl.program_id(1) == 0)
    def _():
        x = x_ref[...]
        ms = jnp.mean(x * x, axis=-1, keepdims=True)
        xn_ref[...] = (x * lax.rsqrt(ms + EPS) * g_ref[...]).astype(BF16)

    y = jnp.dot(xn_ref[...], w_ref[...], preferred_element_type=F32) + b_ref[...]
    if gate:
        y = 1.0 / (1.0 + jnp.exp(-y))
    o_ref[...] = y.astype(o_ref.dtype)


def _proj(x2d, gain, w, bias, gate, tm, tn):
    m, d = x2d.shape
    n = w.shape[1]
    kern = functools.partial(_proj_kernel, gate=gate)
    return pl.pallas_call(
        kern,
        grid=(m // tm, n // tn),
        in_specs=[
            pl.BlockSpec((tm, d), lambda i, j: (i, 0)),
            pl.BlockSpec((1, d), lambda i, j: (0, 0)),
            pl.BlockSpec((d, tn), lambda i, j: (0, j)),
            pl.BlockSpec((1, tn), lambda i, j: (0, j)),
        ],
        out_specs=pl.BlockSpec((tm, tn), lambda i, j: (i, j)),
        out_shape=jax.ShapeDtypeStruct((m, n), BF16),
        scratch_shapes=[pltpu.VMEM((tm, d), BF16)],
        compiler_params=_params(
            ("parallel", "arbitrary"),
            _nbytes((tm, d), F32), _nbytes((d, tn), BF16), _nbytes((tm, tn), BF16),
            scratch_bytes=_nbytes((tm, d), BF16) + _nbytes((tm, d), F32)),
        name="proj",
    )(x2d, gain, w, bias)


def _row_sumsq(x):
    sq = x * x
    hi = sq.astype(BF16)
    lo = (sq - hi.astype(F32)).astype(BF16)
    ones = jnp.ones((x.shape[-1], LANES), BF16)
    return (jnp.dot(hi, ones, preferred_element_type=F32)
            + jnp.dot(lo, ones, preferred_element_type=F32))


def _headnorm_rope_kernel(x_ref, g_ref, *rest, nheads, shifts):
    table_refs, o_ref = rest[:-1], rest[-1]
    tables = [t[...] for t in table_refs]
    for h in range(nheads):
        cols = slice(h * HEAD_DIM, (h + 1) * HEAD_DIM)
        x = x_ref[:, cols].astype(F32)
        ms = _row_sumsq(x) * (1.0 / HEAD_DIM)
        y = x * lax.rsqrt(ms + EPS) * g_ref[:, cols]
        o_ref[:, cols] = _rope(y, tables, shifts).astype(o_ref.dtype)


def _headnorm_rope(p, col_off, width, gains, rope, seq, tm):
    m = p.shape[0]
    assert col_off % width == 0
    tables, shifts = rope
    cb = col_off // width
    sb = seq // tm
    kern = functools.partial(_headnorm_rope_kernel, nheads=width // HEAD_DIM, shifts=shifts)
    tab_spec = pl.BlockSpec((tm, LANES), lambda i: (i % sb, 0))
    return pl.pallas_call(
        kern,
        grid=(m // tm,),
        in_specs=[
            pl.BlockSpec((tm, width), lambda i: (i, cb)),
            pl.BlockSpec((1, width), lambda i: (0, 0)),
        ] + [tab_spec] * len(tables),
        out_specs=pl.BlockSpec((tm, width), lambda i: (i, 0)),
        out_shape=jax.ShapeDtypeStruct((m, width), BF16),
        compiler_params=_params(
            ("parallel",),
            2 * _nbytes((tm, width), BF16), len(tables) * _nbytes((tm, LANES), F32),
            scratch_bytes=4 * _nbytes((tm, LANES), F32)),
        name="headnorm_rope",
    )(p, gains, *tables)


def _mla_prep_kernel(cq_ref, ckv_ref, kpe_ref, wq_ref, wk_ref, wv_ref, gql_ref, gkl_ref,
                     gqh_ref, gkn_ref, gkr_ref, cos_ref, sin_ref,
                     q_ref, k_ref, v_ref):
    tables, shifts = (cos_ref[...], sin_ref[...]), (ROPE_PAIR_SHIFT,)

    def slabs(v, width):
        return jnp.concatenate([v] * (width // LANES), axis=1)

    cq = cq_ref[...].astype(F32)
    inv = lax.rsqrt(_row_sumsq(cq) * (1.0 / MLA_Q_RANK) + EPS)
    cqn = cq * slabs(inv, MLA_Q_RANK) * gql_ref[...]
    q = jnp.dot(cqn.astype(BF16), wq_ref[...], preferred_element_type=F32)
    gqh = gqh_ref[...]
    for h in range(MLA_HEADS):
        nope = slice(h * MLA_QK_PAD, h * MLA_QK_PAD + LANES)
        rope = slice(h * MLA_QK_PAD + LANES, (h + 1) * MLA_QK_PAD)
        inv = lax.rsqrt(_row_sumsq(q[:, h * MLA_QK_PAD:(h + 1) * MLA_QK_PAD]) * (1.0 / MLA_QK) + EPS)
        q_ref[:, nope] = (q[:, nope] * inv * gqh[:, :LANES]).astype(q_ref.dtype)
        q_ref[:, rope] = _rope(q[:, rope] * inv * gqh[:, LANES:], tables, shifts).astype(q_ref.dtype)

    ckv = ckv_ref[...].astype(F32)
    inv = lax.rsqrt(_row_sumsq(ckv) * (1.0 / MLA_KV_RANK) + EPS)
    ckvn = (ckv * slabs(inv, MLA_KV_RANK) * gkl_ref[...]).astype(BF16)
    kn = jnp.dot(ckvn, wk_ref[...], preferred_element_type=F32)
    v_ref[...] = jnp.dot(ckvn, wv_ref[...], preferred_element_type=F32).astype(v_ref.dtype)
    kpe = kpe_ref[...].astype(F32)
    pe_sq = _row_sumsq(kpe)
    gkn, gkr = gkn_ref[...], gkr_ref[...]
    for h in range(MLA_HEADS):
        kh = kn[:, h * MLA_NOPE:(h + 1) * MLA_NOPE]
        inv = lax.rsqrt((_row_sumsq(kh) + pe_sq) * (1.0 / MLA_QK) + EPS)
        k_ref[:, h * MLA_QK_PAD:h * MLA_QK_PAD + LANES] = (kh * inv * gkn).astype(k_ref.dtype)
        k_ref[:, h * MLA_QK_PAD + LANES:(h + 1) * MLA_QK_PAD] = _rope(
            kpe * inv * gkr, tables, shifts).astype(k_ref.dtype)


def _mla_prep(p, off_cq, off_ckv, off_kpe, wq, wk, wv, gql, gkl, gqh, gkn, gkr, tables, seq, tm):
    m = p.shape[0]
    sb = seq // tm
    const = lambda shape: pl.BlockSpec(shape, lambda i: (0, 0))
    tab_spec = pl.BlockSpec((tm, LANES), lambda i: (i % sb, 0))
    qk_cols = MLA_HEADS * MLA_QK_PAD
    v_cols = MLA_HEADS * MLA_V
    return pl.pallas_call(
        _mla_prep_kernel,
        grid=(m // tm,),
        in_specs=[
            pl.BlockSpec((tm, MLA_Q_RANK), lambda i: (i, off_cq // MLA_Q_RANK)),
            pl.BlockSpec((tm, MLA_KV_RANK), lambda i: (i, off_ckv // MLA_KV_RANK)),
            pl.BlockSpec((tm, LANES), lambda i: (i, off_kpe // LANES)),
            const(wq.shape), const(wk.shape), const(wv.shape),
            const(gql.shape), const(gkl.shape), const(gqh.shape), const(gkn.shape), const(gkr.shape),
            tab_spec, tab_spec,
        ],
        out_specs=[
            pl.BlockSpec((tm, qk_cols), lambda i: (i, 0)),
            pl.BlockSpec((tm, qk_cols), lambda i: (i, 0)),
            pl.BlockSpec((tm, v_cols), lambda i: (i, 0)),
        ],
        out_shape=[
            jax.ShapeDtypeStruct((m, qk_cols), BF16),
            jax.ShapeDtypeStruct((m, qk_cols), BF16),
            jax.ShapeDtypeStruct((m, v_cols), BF16),
        ],
        compiler_params=_params(
            ("parallel",),
            _nbytes((tm, MLA_Q_RANK + MLA_KV_RANK + LANES), BF16),
            _nbytes(wq.shape, BF16), _nbytes(wk.shape, BF16), _nbytes(wv.shape, BF16),
            _nbytes((tm, 2 * qk_cols + v_cols), BF16), 2 * _nbytes((tm, LANES), F32),
            scratch_bytes=3 * _nbytes((tm, qk_cols), F32)),
        name="mla_prep",
    )(p, p, p, wq, wk, wv, gql, gkl, gqh, gkn, gkr, *tables)


def _attn_kernel(q_ref, k_ref, v_ref, o_ref, *, sub, heads, dk, dv, shared_kv):
    for h in range(heads):
        kv = 0 if shared_kv else h
        k = k_ref[:, kv * dk:(kv + 1) * dk]
        v = v_ref[:, kv * dv:(kv + 1) * dv]
        v_ones = jnp.concatenate([v, jnp.ones_like(v)], axis=1)
        for r0 in range(0, q_ref.shape[0], sub):
            rows = slice(r0, r0 + sub)
            s = lax.dot_general(q_ref[rows, h * dk:(h + 1) * dk], k, (((1,), (1,)), ((), ())),
                                preferred_element_type=F32)
            p = jnp.exp2(s - jnp.max(s, axis=-1, keepdims=True)).astype(BF16)
            ov = jnp.dot(p, v_ones, preferred_element_type=F32)
            o_ref[rows, h * dv:(h + 1) * dv] = (ov[:, :dv] * (1.0 / ov[:, dv:])).astype(o_ref.dtype)


def _attention(q_arr, k_arr, v_arr, q_off, k_off, v_off, nheads, group, dk, dv, batch, seq):
    hps = ATTN_HEADS_PER_STEP
    shared_kv = group % hps == 0
    assert nheads % hps == 0 and (shared_kv or group == 1)
    kv_w = 1 if shared_kv else hps
    assert q_off % (hps * dk) == 0 and k_off % (kv_w * dk) == 0 and v_off % (kv_w * dv) == 0
    qb, kb, vb = q_off // (hps * dk), k_off // (kv_w * dk), v_off // (kv_w * dv)
    kv_of = (lambda j: (j * hps) // group) if shared_kv else (lambda j: j)
    sub = min(seq, ATTN_SUB_ROWS)
    kern = functools.partial(_attn_kernel, sub=sub, heads=hps, dk=dk, dv=dv, shared_kv=shared_kv)
    return pl.pallas_call(
        kern,
        grid=(batch, nheads // hps),
        in_specs=[
            pl.BlockSpec((seq, hps * dk), lambda b, j: (b, qb + j)),
            pl.BlockSpec((seq, kv_w * dk), lambda b, j: (b, kb + kv_of(j))),
            pl.BlockSpec((seq, kv_w * dv), lambda b, j: (b, vb + kv_of(j))),
        ],
        out_specs=pl.BlockSpec((seq, hps * dv), lambda b, j: (b, j)),
        out_shape=jax.ShapeDtypeStruct((batch * seq, nheads * dv), BF16),
        compiler_params=_params(
            ("parallel", "parallel"),
            _nbytes((seq, hps * dk), BF16), _nbytes((seq, kv_w * dk), BF16),
            _nbytes((seq, kv_w * dv), BF16), _nbytes((seq, hps * dv), BF16),
            scratch_bytes=4 * _nbytes((sub, seq), F32)),
        name="attention",
    )(q_arr, k_arr, v_arr)


def _band_tile(q, k, v, delta, radius):
    s = lax.dot_general(q, k, (((1,), (1,)), ((), ())), preferred_element_type=F32)
    rel = (lax.broadcasted_iota(jnp.int32, s.shape, 0)
           - lax.broadcasted_iota(jnp.int32, s.shape, 1)) + delta
    s = jnp.where(jnp.abs(rel) <= radius, s, NEG_INF)
    m = jnp.max(s, axis=-1, keepdims=True)
    p = jnp.exp(s - m).astype(BF16)
    ov = jnp.dot(p, jnp.concatenate([v, jnp.ones_like(v)], axis=1), preferred_element_type=F32)
    l = ov[:, v.shape[1]:]
    return ov[:, :v.shape[1]] * (1.0 / l), m + jnp.log(l)


def _dilated_kernel(q0_ref, q1_ref, q2_ref, k0_ref, k1_ref, k2_ref, v0_ref, v1_ref, v2_ref,
                    gq_ref, gk_ref, *rest, seq, patterns, shifts):
    n_tab = len(shifts) + 1
    table_refs, out_ref = rest[:n_tab], rest[n_tab]
    qb_ref, kb_ref, qf_ref, kf_ref, vf_ref, of_ref, lf_ref = rest[n_tab + 1:]
    q_refs, k_refs, v_refs = (q0_ref, q1_ref, q2_ref), (k0_ref, k1_ref, k2_ref), (v0_ref, v1_ref, v2_ref)
    strided = [g for g, (_, dil) in enumerate(patterns) if dil > 1]
    dense = [g for g, (_, dil) in enumerate(patterns) if dil == 1]

    def prepared(src, gain, rows):
        x = src[rows, :].astype(F32)
        ms = _row_sumsq(x) * (1.0 / HEAD_DIM)
        return _rope(x * lax.rsqrt(ms + EPS) * gain[...], [t[rows, :] for t in table_refs], shifts)

    for g, (window, dil) in enumerate(patterns):
        radius = window // (2 * dil)
        length = seq // dil
        kw = min(length, QBLOCK + 2 * radius)
        slot = strided.index(g) if dil > 1 else dense.index(g)
        for r0 in range(0, seq, DIL_PREP_ROWS):
            chunk = slice(r0, r0 + DIL_PREP_ROWS)
            if dil > 1:
                qf_ref[slot, chunk, :] = prepared(q_refs[g], gq_ref, chunk)
                kf_ref[slot, chunk, :] = prepared(k_refs[g], gk_ref, chunk)
                vf_ref[slot, chunk, :] = v_refs[g][chunk, :].astype(F32)
            else:
                qb_ref[slot, chunk, :] = prepared(q_refs[g], gq_ref, chunk).astype(BF16)
                kb_ref[slot, chunk, :] = prepared(k_refs[g], gk_ref, chunk).astype(BF16)
        for r in range(dil):
            def rows(start, n, r=r, dil=dil):
                return pl.ds(start, n) if dil == 1 else pl.ds(start * dil + r, n, stride=dil)

            def window_of(dense_ref, copy, start, n, dil=dil, slot=slot, rows=rows):
                if dil == 1:
                    return dense_ref[rows(start, n), :]
                return copy[slot, rows(start, n), :].astype(BF16)

            for t in range(length // QBLOCK):
                q0 = t * QBLOCK
                ks = min(max(q0 - radius, 0), length - kw)
                o, lse = _band_tile(
                    window_of(qb_ref.at[slot] if dil == 1 else None, qf_ref, q0, QBLOCK),
                    window_of(kb_ref.at[slot] if dil == 1 else None, kf_ref, ks, kw),
                    window_of(v_refs[g], vf_ref, ks, kw), q0 - ks, radius)
                of_ref[g, rows(q0, QBLOCK), :] = o
                lf_ref[g, rows(q0, QBLOCK), :] = lse

    la, lb, lc = lf_ref[0], lf_ref[1], lf_ref[2]
    m = jnp.maximum(jnp.maximum(la, lb), lc)
    ea, eb, ec = jnp.exp(la - m), jnp.exp(lb - m), jnp.exp(lc - m)
    num = ea * of_ref[0] + eb * of_ref[1] + ec * of_ref[2]
    out_ref[...] = (num * (1.0 / (ea + eb + ec))).astype(out_ref.dtype)


def _dilated(p, off_q, off_k, off_v, gq, gk, rope, batch, seq):
    ng = len(DIL_PATTERNS)
    n_strided = sum(dil > 1 for _, dil in DIL_PATTERNS)
    tables, shifts = rope
    kern = functools.partial(_dilated_kernel, seq=seq, patterns=DIL_PATTERNS, shifts=shifts)
    group_w = DIL_HPG * HEAD_DIM

    def head(off, g):
        assert off % HEAD_DIM == 0
        cb = (off + g * group_w) // HEAD_DIM
        return pl.BlockSpec((seq, HEAD_DIM), lambda b, h: (b, cb + h))

    const = lambda shape: pl.BlockSpec(shape, lambda b, h: (0, 0))
    slab = _nbytes((seq, HEAD_DIM), F32)
    return pl.pallas_call(
        kern,
        grid=(batch, DIL_HPG),
        in_specs=([head(off_q, g) for g in range(ng)] + [head(off_k, g) for g in range(ng)]
                  + [head(off_v, g) for g in range(ng)]
                  + [const((1, HEAD_DIM))] * 2 + [const((seq, LANES))] * len(tables)),
        out_specs=pl.BlockSpec((seq, HEAD_DIM), lambda b, h: (b, h)),
        out_shape=jax.ShapeDtypeStruct((batch * seq, group_w), BF16),
        scratch_shapes=[pltpu.VMEM((ng - n_strided, seq, HEAD_DIM), BF16)] * 2
                       + [pltpu.VMEM((n_strided, seq, HEAD_DIM), F32)] * 3
                       + [pltpu.VMEM((ng, seq, HEAD_DIM), F32)] * 2,
        compiler_params=_params(
            ("parallel", "parallel"),
            (3 * ng + 1) * _nbytes((seq, HEAD_DIM), BF16), len(tables) * slab,
            scratch_bytes=(ng - n_strided + 3 * n_strided + 2 * ng) * slab + 4 * slab),
        name="dilated",
    )(*([p] * (3 * ng)), gq, gk, *tables)


def _merge_kernel(oa_ref, ob_ref, oc_ref, ga_ref, gb_ref, gc_ref, wa_ref, wb_ref, wc_ref, o_ref):
    ya = jnp.dot(oa_ref[...], wa_ref[...], preferred_element_type=F32)
    yb = jnp.dot(ob_ref[...], wb_ref[...], preferred_element_type=F32)
    yc = jnp.dot(oc_ref[...], wc_ref[...], preferred_element_type=F32)
    merged = (ga_ref[...].astype(F32) * ya + gb_ref[...].astype(F32) * yb
              + gc_ref[...].astype(F32) * yc)
    o_ref[...] = merged.astype(o_ref.dtype)


def _merge(oa, ob, oc, p, w_oa, w_ob, w_oc, tm, tn):
    m = oa.shape[0]
    d = w_oa.shape[1]
    nj = d // tn
    row = lambda a: pl.BlockSpec((tm, a.shape[1]), lambda i, j: (i, 0))
    wcol = lambda w: pl.BlockSpec((w.shape[0], tn), lambda i, j: (0, j))
    gate = lambda g: pl.BlockSpec((tm, tn), lambda i, j: (i, g * nj + j))
    return pl.pallas_call(
        _merge_kernel,
        grid=(m // tm, nj),
        in_specs=[row(oa), row(ob), row(oc), gate(0), gate(1), gate(2),
                  wcol(w_oa), wcol(w_ob), wcol(w_oc)],
        out_specs=pl.BlockSpec((tm, tn), lambda i, j: (i, j)),
        out_shape=jax.ShapeDtypeStruct((m, d), BF16),
        compiler_params=_params(
            ("parallel", "parallel"),
            _nbytes((tm, oa.shape[1] + ob.shape[1] + oc.shape[1]), BF16),
            4 * _nbytes((tm, tn), BF16),
            _nbytes((w_oa.shape[0] + w_ob.shape[0] + w_oc.shape[0], tn), BF16),
            scratch_bytes=4 * _nbytes((tm, tn), F32)),
        name="merge",
    )(oa, ob, oc, p, p, p, w_oa, w_ob, w_oc)


def _resid_matmul_kernel(x_ref, a_ref, w_ref, o_ref):
    o_ref[...] = x_ref[...] + jnp.dot(a_ref[...], w_ref[...], preferred_element_type=F32)


def _resid_matmul(x2d, a, w, tm, tn):
    m, d = x2d.shape
    k = a.shape[1]
    return pl.pallas_call(
        _resid_matmul_kernel,
        grid=(m // tm, d // tn),
        in_specs=[
            pl.BlockSpec((tm, tn), lambda i, j: (i, j)),
            pl.BlockSpec((tm, k), lambda i, j: (i, 0)),
            pl.BlockSpec((k, tn), lambda i, j: (0, j)),
        ],
        out_specs=pl.BlockSpec((tm, tn), lambda i, j: (i, j)),
        out_shape=jax.ShapeDtypeStruct((m, d), F32),
        compiler_params=_params(
            ("parallel", "parallel"),
            2 * _nbytes((tm, tn), F32), _nbytes((tm, k), BF16), _nbytes((k, tn), BF16)),
        name="out_proj",
    )(x2d, a, w)


def _mlp_kernel(x_ref, g_ref, wu_ref, wd_ref, o_ref, xn_ref):
    c = pl.program_id(1)

    @pl.when(c == 0)
    def _():
        x = x_ref[...]
        ms = jnp.mean(x * x, axis=-1, keepdims=True)
        xn_ref[...] = (x * lax.rsqrt(ms + EPS) * g_ref[...]).astype(BF16)
        o_ref[...] = x

    h = jnp.dot(xn_ref[...], wu_ref[...], preferred_element_type=F32)
    h = jnp.square(jnp.maximum(h, 0.0)).astype(BF16)
    o_ref[...] += jnp.dot(h, wd_ref[...], preferred_element_type=F32)


def _mlp(x2d, gain, w_up, w_down, tm, tf):
    m, d = x2d.shape
    f = w_up.shape[1]
    return pl.pallas_call(
        _mlp_kernel,
        grid=(m // tm, f // tf),
        in_specs=[
            pl.BlockSpec((tm, d), lambda i, c: (i, 0)),
            pl.BlockSpec((1, d), lambda i, c: (0, 0)),
            pl.BlockSpec((d, tf), lambda i, c: (0, c)),
            pl.BlockSpec((tf, d), lambda i, c: (c, 0)),
        ],
        out_specs=pl.BlockSpec((tm, d), lambda i, c: (i, 0)),
        out_shape=jax.ShapeDtypeStruct((m, d), F32),
        scratch_shapes=[pltpu.VMEM((tm, d), BF16)],
        compiler_params=_params(
            ("parallel", "arbitrary"),
            2 * _nbytes((tm, d), F32), 2 * _nbytes((d, tf), BF16),
            scratch_bytes=_nbytes((tm, d), BF16) + _nbytes((tm, tf), F32)),
        name="mlp",
    )(x2d, gain, w_up, w_down)


def _largest_tile(n, cap):
    t = min(n, cap)
    while n % t:
        t //= 2
    return t


def kernel(x, attn_norm, w_in, b_gate, mla_q_lat_norm, w_uq, mla_kv_lat_norm, w_ukv,
           mla_q_head_norm, mla_k_head_norm, gqa_q_norm, gqa_k_norm, dil_q_norm, dil_k_norm,
           w_oa, w_ob, w_oc, w_out, mlp_norm, w_up, w_down):
    batch, seq, d = x.shape
    depth = w_in.shape[0]
    m = batch * seq
    d_ff = w_up.shape[2]
    gate_cols = 3 * d
    assert w_in.shape[2] == A_COLS + B_COLS + C_COLS + gate_cols
    assert seq % GRID_W == 0 and seq % (16 * QBLOCK) == 0 and d % LANES == 0
    tn_d = _largest_tile(d, 1024)

    tn_proj = 1024
    nq_b, nk_b = GQA_HEADS * HEAD_DIM, GQA_KV_HEADS * HEAD_DIM
    nh_c = DIL_HEADS * HEAD_DIM
    off_bq = 0
    off_bk = off_bq + nq_b
    off_bv = off_bk + nk_b
    off_cq = off_bv + nk_b
    off_ck = off_cq + nh_c
    off_cv = off_ck + nh_c
    off_aq = off_cv + nh_c
    off_akv = off_aq + MLA_Q_RANK
    off_ape = off_akv + MLA_KV_RANK
    n_proj = _round_up(off_ape + LANES, tn_proj)

    tm_big = _largest_tile(m, 1024)
    tm_seq = _largest_tile(seq, 1024)
    tm_mla = _largest_tile(seq, 512)

    pos = jnp.arange(seq, dtype=F32)
    row = jnp.repeat(jnp.arange(seq // GRID_W, dtype=F32), GRID_W)
    col = jnp.tile(jnp.arange(GRID_W, dtype=F32), seq // GRID_W)
    half_mla, half_ax, half_dil = MLA_ROPE // 2, HEAD_DIM // 4, PARTIAL_ROPE_DIM // 2
    rope_mla = _rope_tables(seq, ROPE_PAIR_SHIFT, [(0, half_mla, pos, MLA_ROPE_THETA)])
    rope_gqa = _rope_tables(seq, half_ax, [(0, half_ax, row, AXIAL_THETA),
                                           (2 * half_ax, half_ax, col, AXIAL_THETA)])
    rope_dil = _rope_tables(seq, half_dil, [(0, half_dil, pos, PARTIAL_ROPE_THETA)])

    def rope_slab(w):
        zeros = jnp.zeros(w.shape[:-1] + (ROPE_PAIR_SHIFT - half_mla,), w.dtype)
        return jnp.concatenate([w[..., :half_mla], zeros, w[..., half_mla:], zeros], axis=-1)

    x2d = x.reshape(m, d)
    for l in range(depth):
        wl = w_in[l]
        c1 = A_COLS + B_COLS + C_COLS
        w_p = jnp.concatenate([
            wl[:, A_COLS:c1],
            wl[:, :MLA_Q_RANK + MLA_KV_RANK],
            rope_slab(wl[:, MLA_Q_RANK + MLA_KV_RANK:A_COLS]),
            jnp.zeros((d, n_proj - off_ape - LANES), F32),
        ], axis=1).astype(BF16)
        w_g = wl[:, c1:].astype(BF16)
        gain = attn_norm[l][None, :]
        p = _proj(x2d, gain, w_p, jnp.zeros((1, n_proj), F32), False, tm_big, tn_proj)
        gates = _proj(x2d, gain, w_g, b_gate[l][None, :], True, tm_big,
                      _largest_tile(gate_cols, 1024))

        wq = w_uq[l].reshape(MLA_Q_RANK, MLA_HEADS, MLA_QK)
        wq = jnp.concatenate([wq[..., :MLA_NOPE], rope_slab(wq[..., MLA_NOPE:])], axis=-1
                             ).reshape(MLA_Q_RANK, MLA_HEADS * MLA_QK_PAD).astype(BF16)
        wkv = w_ukv[l].reshape(MLA_KV_RANK, MLA_HEADS, MLA_NOPE + MLA_V)
        wk = wkv[:, :, :MLA_NOPE].reshape(MLA_KV_RANK, MLA_HEADS * MLA_NOPE).astype(BF16)
        wv = wkv[:, :, MLA_NOPE:].reshape(MLA_KV_RANK, MLA_HEADS * MLA_V).astype(BF16)
        gqh = mla_q_head_norm[l] * (LOG2_E * MLA_QK ** -0.5)
        gqh = jnp.concatenate([gqh[:MLA_NOPE], rope_slab(gqh[MLA_NOPE:])])[None, :]
        gkn = mla_k_head_norm[l][None, :MLA_NOPE]
        gkr = rope_slab(mla_k_head_norm[l][MLA_NOPE:])[None, :]
        qa, ka, va = _mla_prep(p, off_aq, off_akv, off_ape, wq, wk, wv,
                               mla_q_lat_norm[l][None, :], mla_kv_lat_norm[l][None, :],
                               gqh, gkn, gkr, rope_mla[0], seq, tm_mla)
        oa = _attention(qa, ka, va, 0, 0, 0, MLA_HEADS, 1, MLA_QK_PAD, MLA_V, batch, seq)

        scale = HEAD_DIM ** -0.5
        gq = jnp.tile(gqa_q_norm[l] * (LOG2_E * scale), GQA_HEADS)[None, :]
        gk = jnp.tile(gqa_k_norm[l], GQA_KV_HEADS)[None, :]
        qb = _headnorm_rope(p, off_bq, nq_b, gq, rope_gqa, seq, tm_seq)
        kb = _headnorm_rope(p, off_bk, nk_b, gk, rope_gqa, seq, tm_seq)
        ob = _attention(qb, kb, p, 0, 0, off_bv, GQA_HEADS, GQA_GROUP, HEAD_DIM, HEAD_DIM,
                        batch, seq)

        oc = _dilated(p, off_cq, off_ck, off_cv, (dil_q_norm[l] * scale)[None, :],
                      dil_k_norm[l][None, :], rope_dil, batch, seq)

        merged = _merge(oa, ob, oc, gates, w_oa[l].astype(BF16), w_ob[l].astype(BF16),
                        w_oc[l].astype(BF16), tm_big, tn_d)
        x2d = _resid_matmul(x2d, merged, w_out[l].astype(BF16), tm_big, tn_d)
        x2d = _mlp(x2d, mlp_norm[l][None, :], w_up[l].astype(BF16), w_down[l].astype(BF16),
                   _largest_tile(m, 1024), _largest_tile(d_ff, 512))
    return x2d.reshape(batch, seq, d)
```

```python
import functools

import jax
import jax.numpy as jnp
from jax import lax
from jax.experimental import pallas as pl
from jax.experimental.pallas import tpu as pltpu

F32 = jnp.float32
BF16 = jnp.bfloat16

LANES = 128
VMEM_CAP_BYTES = 60000 * 1024

HEAD_DIM = 128
GRID_W = 64
EPS = 1e-6
NEG_INF = -1e30

MLA_HEADS = 8
MLA_Q_RANK = 512
MLA_KV_RANK = 256
MLA_NOPE = 128
MLA_ROPE = 64
MLA_V = 128
MLA_ROPE_THETA = 10000.0
MLA_QK = MLA_NOPE + MLA_ROPE
MLA_QK_PAD = 2 * LANES

GQA_HEADS = 8
GQA_KV_HEADS = 2
GQA_GROUP = GQA_HEADS // GQA_KV_HEADS
AXIAL_THETA = 10000.0

DIL_PATTERNS = ((128, 1), (512, 4), (2048, 16))
DIL_HPG = 4
DIL_HEADS = DIL_HPG * len(DIL_PATTERNS)
PARTIAL_ROPE_DIM = HEAD_DIM // 4
PARTIAL_ROPE_THETA = 500000.0
QBLOCK = 128
ATTN_SUB_ROWS = 256
LOG2_E = 1.4426950408889634
ATTN_HEADS_PER_STEP = 2
DIL_PREP_ROWS = 256

A_COLS = MLA_Q_RANK + MLA_KV_RANK + MLA_ROPE
B_COLS = (GQA_HEADS + 2 * GQA_KV_HEADS) * HEAD_DIM
C_COLS = 3 * DIL_HEADS * HEAD_DIM


def _round_up(n, m):
    return -(-n // m) * m


def _params(semantics, *block_bytes, scratch_bytes=0):
    need = 2 * sum(block_bytes) + scratch_bytes
    limit = min(VMEM_CAP_BYTES, need + (24 << 20))
    return pltpu.CompilerParams(dimension_semantics=semantics, vmem_limit_bytes=limit)


def _nbytes(shape, dtype):
    n = 1
    for s in shape:
        n *= s
    return n * jnp.dtype(dtype).itemsize


ROPE_PAIR_SHIFT = LANES // 2


def _rope_tables(seq, dist, segments):
    cos = jnp.ones((seq, LANES), F32)
    sin_x1 = jnp.zeros((seq, LANES), F32)
    sin_x2 = jnp.zeros((seq, LANES), F32)
    for first, half, pos, theta in segments:
        inv = theta ** (-jnp.arange(half, dtype=F32) / half)
        ang = pos[:, None] * inv[None, :]
        c, s = jnp.cos(ang), jnp.sin(ang)
        hi = first + dist
        cos = cos.at[:, first:first + half].set(c).at[:, hi:hi + half].set(c)
        sin_x1 = sin_x1.at[:, first:first + half].set(-s)
        sin_x2 = sin_x2.at[:, hi:hi + half].set(s)
    if 2 * dist == LANES:
        return (cos, sin_x1 + sin_x2), (dist,)
    return (cos, sin_x1, sin_x2), (LANES - dist, dist)


def _rope(x, tables, shifts):
    out = x * tables[0]
    for table, shift in zip(tables[1:], shifts):
        out = out + pltpu.roll(x, shift, 1) * table
    return out


def _proj_kernel(x_ref, g_ref, w_ref, b_ref, o_ref, xn_ref, *, gate):
    @pl.when(pl.program_id(1) == 0)
    def _():
        x = x_ref[...]
        ms = jnp.mean(x * x, axis=-1, keepdims=True)
        xn_ref[...] = (x * lax.rsqrt(ms + EPS) * g_ref[...]).astype(BF16)

    y = jnp.dot(xn_ref[...], w_ref[...], preferred_element_type=F32) + b_ref[...]
    if gate:
        y = 1.0 / (1.0 + jnp.exp(-y))
    o_ref[...] = y.astype(o_ref.dtype)


def _proj(x2d, gain, w, bias, gate, tm, tn):
    m, d = x2d.shape
    n = w.shape[1]
    kern = functools.partial(_proj_kernel, gate=gate)
    return pl.pallas_call(
        kern,
        grid=(m // tm, n // tn),
        in_specs=[
            pl.BlockSpec((tm, d), lambda i, j: (i, 0)),
            pl.BlockSpec((1, d), lambda i, j: (0, 0)),
            pl.BlockSpec((d, tn), lambda i, j: (0, j)),
            pl.BlockSpec((1, tn), lambda i, j: (0, j)),
        ],
        out_specs=pl.BlockSpec((tm, tn), lambda i, j: (i, j)),
        out_shape=jax.ShapeDtypeStruct((m, n), BF16),
        scratch_shapes=[pltpu.VMEM((tm, d), BF16)],
        compiler_params=_params(
            ("parallel", "arbitrary"),
            _nbytes((tm, d), F32), _nbytes((d, tn), BF16), _nbytes((tm, tn), BF16),
            scratch_bytes=_nbytes((tm, d), BF16) + _nbytes((tm, d), F32)),
        name="proj",
    )(x2d, gain, w, bias)


def _row_sumsq(x):
    sq = x * x
    hi = sq.astype(BF16)
    lo = (sq - hi.astype(F32)).astype(BF16)
    ones = jnp.ones((x.shape[-1], LANES), BF16)
    return (jnp.dot(hi, ones, preferred_element_type=F32)
            + jnp.dot(lo, ones, preferred_element_type=F32))


def _headnorm_rope_kernel(x_ref, g_ref, *rest, nheads, shifts):
    table_refs, o_ref = rest[:-1], rest[-1]
    tables = [t[...] for t in table_refs]
    for h in range(nheads):
        cols = slice(h * HEAD_DIM, (h + 1) * HEAD_DIM)
        x = x_ref[:, cols].astype(F32)
        ms = _row_sumsq(x) * (1.0 / HEAD_DIM)
        y = x * lax.rsqrt(ms + EPS) * g_ref[:, cols]
        o_ref[:, cols] = _rope(y, tables, shifts).astype(o_ref.dtype)


def _headnorm_rope(p, col_off, width, gains, rope, seq, tm):
    m = p.shape[0]
    assert col_off % width == 0
    tables, shifts = rope
    cb = col_off // width
    sb = seq // tm
    kern = functools.partial(_headnorm_rope_kernel, nheads=width // HEAD_DIM, shifts=shifts)
    tab_spec = pl.BlockSpec((tm, LANES), lambda i: (i % sb, 0))
    return pl.pallas_call(
        kern,
        grid=(m // tm,),
        in_specs=[
            pl.BlockSpec((tm, width), lambda i: (i, cb)),
            pl.BlockSpec((1, width), lambda i: (0, 0)),
        ] + [tab_spec] * len(tables),
        out_specs=pl.BlockSpec((tm, width), lambda i: (i, 0)),
        out_shape=jax.ShapeDtypeStruct((m, width), BF16),
        compiler_params=_params(
            ("parallel",),
            2 * _nbytes((tm, width), BF16), len(tables) * _nbytes((tm, LANES), F32),
            scratch_bytes=4 * _nbytes((tm, LANES), F32)),
        name="headnorm_rope",
    )(p, gains, *tables)


def _mla_prep_kernel(cq_ref, ckv_ref, kpe_ref, wq_ref, wk_ref, wv_ref, gql_ref, gkl_ref,
                     gqh_ref, gkn_ref, gkr_ref, cos_ref, sin_ref,
                     q_ref, k_ref, v_ref):
    tables, shifts = (cos_ref[...], sin_ref[...]), (ROPE_PAIR_SHIFT,)
    sumsq = lambda t: jnp.sum(t * t, axis=-1, keepdims=True)

    cq = cq_ref[...].astype(F32)
    cqn = cq * lax.rsqrt(sumsq(cq) * (1.0 / MLA_Q_RANK) + EPS) * gql_ref[...]
    q = jnp.dot(cqn.astype(BF16), wq_ref[...], preferred_element_type=F32)
    gqh = gqh_ref[...]
    for h in range(MLA_HEADS):
        nope = slice(h * MLA_QK_PAD, h * MLA_QK_PAD + LANES)
        rope = slice(h * MLA_QK_PAD + LANES, (h + 1) * MLA_QK_PAD)
        inv = lax.rsqrt(sumsq(q[:, h * MLA_QK_PAD:(h + 1) * MLA_QK_PAD]) * (1.0 / MLA_QK) + EPS)
        q_ref[:, nope] = (q[:, nope] * inv * gqh[:, :LANES]).astype(q_ref.dtype)
        q_ref[:, rope] = _rope(q[:, rope] * inv * gqh[:, LANES:], tables, shifts).astype(q_ref.dtype)

    ckv = ckv_ref[...].astype(F32)
    ckvn = (ckv * lax.rsqrt(sumsq(ckv) * (1.0 / MLA_KV_RANK) + EPS) * gkl_ref[...]).astype(BF16)
    kn = jnp.dot(ckvn, wk_ref[...], preferred_element_type=F32)
    v_ref[...] = jnp.dot(ckvn, wv_ref[...], preferred_element_type=F32).astype(v_ref.dtype)
    kpe = kpe_ref[...].astype(F32)
    pe_sq = sumsq(kpe)
    gkn, gkr = gkn_ref[...], gkr_ref[...]
    for h in range(MLA_HEADS):
        kh = kn[:, h * MLA_NOPE:(h + 1) * MLA_NOPE]
        inv = lax.rsqrt((sumsq(kh) + pe_sq) * (1.0 / MLA_QK) + EPS)
        k_ref[:, h * MLA_QK_PAD:h * MLA_QK_PAD + LANES] = (kh * inv * gkn).astype(k_ref.dtype)
        k_ref[:, h * MLA_QK_PAD + LANES:(h + 1) * MLA_QK_PAD] = _rope(
            kpe * inv * gkr, tables, shifts).astype(k_ref.dtype)


def _mla_prep(p, off_cq, off_ckv, off_kpe, wq, wk, wv, gql, gkl, gqh, gkn, gkr, tables, seq, tm):
    m = p.shape[0]
    sb = seq // tm
    const = lambda shape: pl.BlockSpec(shape, lambda i: (0, 0))
    tab_spec = pl.BlockSpec((tm, LANES), lambda i: (i % sb, 0))
    qk_cols = MLA_HEADS * MLA_QK_PAD
    v_cols = MLA_HEADS * MLA_V
    return pl.pallas_call(
        _mla_prep_kernel,
        grid=(m // tm,),
        in_specs=[
            pl.BlockSpec((tm, MLA_Q_RANK), lambda i: (i, off_cq // MLA_Q_RANK)),
            pl.BlockSpec((tm, MLA_KV_RANK), lambda i: (i, off_ckv // MLA_KV_RANK)),
            pl.BlockSpec((tm, LANES), lambda i: (i, off_kpe // LANES)),
            const(wq.shape), const(wk.shape), const(wv.shape),
            const(gql.shape), const(gkl.shape), const(gqh.shape), const(gkn.shape), const(gkr.shape),
            tab_spec, tab_spec,
        ],
        out_specs=[
            pl.BlockSpec((tm, qk_cols), lambda i: (i, 0)),
            pl.BlockSpec((tm, qk_cols), lambda i: (i, 0)),
            pl.BlockSpec((tm, v_cols), lambda i: (i, 0)),
        ],
        out_shape=[
            jax.ShapeDtypeStruct((m, qk_cols), BF16),
            jax.ShapeDtypeStruct((m, qk_cols), BF16),
            jax.ShapeDtypeStruct((m, v_cols), BF16),
        ],
        compiler_params=_params(
            ("parallel",),
            _nbytes((tm, MLA_Q_RANK + MLA_KV_RANK + LANES), BF16),
            _nbytes(wq.shape, BF16), _nbytes(wk.shape, BF16), _nbytes(wv.shape, BF16),
            _nbytes((tm, 2 * qk_cols + v_cols), BF16), 2 * _nbytes((tm, LANES), F32),
            scratch_bytes=3 * _nbytes((tm, qk_cols), F32)),
        name="mla_prep",
    )(p, p, p, wq, wk, wv, gql, gkl, gqh, gkn, gkr, *tables)


def _attn_kernel(q_ref, k_ref, v_ref, o_ref, *, sub, heads, dk, dv, shared_kv):
    for h in range(heads):
        kv = 0 if shared_kv else h
        k = k_ref[:, kv * dk:(kv + 1) * dk]
        v = v_ref[:, kv * dv:(kv + 1) * dv]
        v_ones = jnp.concatenate([v, jnp.ones_like(v)], axis=1)
        for r0 in range(0, q_ref.shape[0], sub):
            rows = slice(r0, r0 + sub)
            s = lax.dot_general(q_ref[rows, h * dk:(h + 1) * dk], k, (((1,), (1,)), ((), ())),
                                preferred_element_type=F32)
            p = jnp.exp2(s - jnp.max(s, axis=-1, keepdims=True)).astype(BF16)
            ov = jnp.dot(p, v_ones, preferred_element_type=F32)
            o_ref[rows, h * dv:(h + 1) * dv] = (ov[:, :dv] * (1.0 / ov[:, dv:])).astype(o_ref.dtype)


def _attention(q_arr, k_arr, v_arr, q_off, k_off, v_off, nheads, group, dk, dv, batch, seq):
    hps = ATTN_HEADS_PER_STEP
    shared_kv = group % hps == 0
    assert nheads % hps == 0 and (shared_kv or group == 1)
    kv_w = 1 if shared_kv else hps
    assert q_off % (hps * dk) == 0 and k_off % (kv_w * dk) == 0 and v_off % (kv_w * dv) == 0
    qb, kb, vb = q_off // (hps * dk), k_off // (kv_w * dk), v_off // (kv_w * dv)
    kv_of = (lambda j: (j * hps) // group) if shared_kv else (lambda j: j)
    sub = min(seq, ATTN_SUB_ROWS)
    kern = functools.partial(_attn_kernel, sub=sub, heads=hps, dk=dk, dv=dv, shared_kv=shared_kv)
    return pl.pallas_call(
        kern,
        grid=(batch, nheads // hps),
        in_specs=[
            pl.BlockSpec((seq, hps * dk), lambda b, j: (b, qb + j)),
            pl.BlockSpec((seq, kv_w * dk), lambda b, j: (b, kb + kv_of(j))),
            pl.BlockSpec((seq, kv_w * dv), lambda b, j: (b, vb + kv_of(j))),
        ],
        out_specs=pl.BlockSpec((seq, hps * dv), lambda b, j: (b, j)),
        out_shape=jax.ShapeDtypeStruct((batch * seq, nheads * dv), BF16),
        compiler_params=_params(
            ("parallel", "parallel"),
            _nbytes((seq, hps * dk), BF16), _nbytes((seq, kv_w * dk), BF16),
            _nbytes((seq, kv_w * dv), BF16), _nbytes((seq, hps * dv), BF16),
            scratch_bytes=4 * _nbytes((sub, seq), F32)),
        name="attention",
    )(q_arr, k_arr, v_arr)


def _dilated_kernel(q0_ref, q1_ref, q2_ref, k0_ref, k1_ref, k2_ref, v0_ref, v1_ref, v2_ref,
                    gq_ref, gk_ref, *rest, seq, patterns, shifts):
    n_tab = len(shifts) + 1
    table_refs, out_ref = rest[:n_tab], rest[n_tab]
    qb_ref, kb_ref, qf_ref, kf_ref, vf_ref, of_ref, lf_ref, s_ref, p_ref, m_ref = rest[n_tab + 1:]
    q_refs, k_refs, v_refs = (q0_ref, q1_ref, q2_ref), (k0_ref, k1_ref, k2_ref), (v0_ref, v1_ref, v2_ref)
    strided = [g for g, (_, dil) in enumerate(patterns) if dil > 1]
    dense = [g for g, (_, dil) in enumerate(patterns) if dil == 1]

    def prepared(src, gain, rows):
        x = src[rows, :].astype(F32)
        ms = _row_sumsq(x) * (1.0 / HEAD_DIM)
        return _rope(x * lax.rsqrt(ms + EPS) * gain[...], [t[rows, :] for t in table_refs], shifts)

    for g, (window, dil) in enumerate(patterns):
        radius = window // (2 * dil)
        length = seq // dil
        kw = min(length, QBLOCK + 2 * radius)
        slot = strided.index(g) if dil > 1 else dense.index(g)
        for r0 in range(0, seq, DIL_PREP_ROWS):
            chunk = slice(r0, r0 + DIL_PREP_ROWS)
            if dil > 1:
                qf_ref[slot, chunk, :] = prepared(q_refs[g], gq_ref, chunk)
                kf_ref[slot, chunk, :] = prepared(k_refs[g], gk_ref, chunk)
                vf_ref[slot, chunk, :] = v_refs[g][chunk, :].astype(F32)
            else:
                qb_ref[slot, chunk, :] = prepared(q_refs[g], gq_ref, chunk).astype(BF16)
                kb_ref[slot, chunk, :] = prepared(k_refs[g], gk_ref, chunk).astype(BF16)
        def rows(r, start, n, dil=dil):
            return pl.ds(start, n) if dil == 1 else pl.ds(start * dil + r, n, stride=dil)

        def window_of(dense_ref, copy, r, start, n, dil=dil, slot=slot, rows=rows):
            if dil == 1:
                return dense_ref[rows(r, start, n), :]
            return copy[slot, rows(r, start, n), :].astype(BF16)

        tiles = [(r, t * QBLOCK, min(max(t * QBLOCK - radius, 0), length - kw))
                 for r in range(dil) for t in range(length // QBLOCK)]
        for i, (r, q0, ks) in enumerate(tiles):
            q = window_of(qb_ref.at[slot] if dil == 1 else None, qf_ref, r, q0, QBLOCK)
            k = window_of(kb_ref.at[slot] if dil == 1 else None, kf_ref, r, ks, kw)
            s = lax.dot_general(q, k, (((1,), (1,)), ((), ())), preferred_element_type=F32)
            rel = (lax.broadcasted_iota(jnp.int32, s.shape, 0)
                   - lax.broadcasted_iota(jnp.int32, s.shape, 1)) + (q0 - ks)
            s_ref[i, :, :kw] = jnp.where(jnp.abs(rel) <= radius, s, NEG_INF)
        for i in range(len(tiles)):
            s = s_ref[i, :, :kw]
            m = jnp.max(s, axis=-1, keepdims=True)
            p_ref[i, :, :kw] = jnp.exp(s - m).astype(BF16)
            m_ref[i] = jnp.broadcast_to(m, (QBLOCK, HEAD_DIM))
        for i, (r, q0, ks) in enumerate(tiles):
            v = window_of(v_refs[g], vf_ref, r, ks, kw)
            ov = jnp.dot(p_ref[i, :, :kw], jnp.concatenate([v, jnp.ones_like(v)], axis=1),
                         preferred_element_type=F32)
            l = ov[:, HEAD_DIM:]
            of_ref[g, rows(r, q0, QBLOCK), :] = ov[:, :HEAD_DIM] * (1.0 / l)
            lf_ref[g, rows(r, q0, QBLOCK), :] = m_ref[i] + jnp.log(l)

    la, lb, lc = lf_ref[0], lf_ref[1], lf_ref[2]
    m = jnp.maximum(jnp.maximum(la, lb), lc)
    ea, eb, ec = jnp.exp(la - m), jnp.exp(lb - m), jnp.exp(lc - m)
    num = ea * of_ref[0] + eb * of_ref[1] + ec * of_ref[2]
    out_ref[...] = (num * (1.0 / (ea + eb + ec))).astype(out_ref.dtype)


def _dilated(p, off_q, off_k, off_v, gq, gk, rope, batch, seq):
    ng = len(DIL_PATTERNS)
    n_strided = sum(dil > 1 for _, dil in DIL_PATTERNS)
    tables, shifts = rope
    kw_max = max(min(seq // dil, QBLOCK + window // dil) for window, dil in DIL_PATTERNS)
    kern = functools.partial(_dilated_kernel, seq=seq, patterns=DIL_PATTERNS, shifts=shifts)
    group_w = DIL_HPG * HEAD_DIM

    def head(off, g):
        assert off % HEAD_DIM == 0
        cb = (off + g * group_w) // HEAD_DIM
        return pl.BlockSpec((seq, HEAD_DIM), lambda b, h: (b, cb + h))

    const = lambda shape: pl.BlockSpec(shape, lambda b, h: (0, 0))
    slab = _nbytes((seq, HEAD_DIM), F32)
    return pl.pallas_call(
        kern,
        grid=(batch, DIL_HPG),
        in_specs=([head(off_q, g) for g in range(ng)] + [head(off_k, g) for g in range(ng)]
                  + [head(off_v, g) for g in range(ng)]
                  + [const((1, HEAD_DIM))] * 2 + [const((seq, LANES))] * len(tables)),
        out_specs=pl.BlockSpec((seq, HEAD_DIM), lambda b, h: (b, h)),
        out_shape=jax.ShapeDtypeStruct((batch * seq, group_w), BF16),
        scratch_shapes=[pltpu.VMEM((ng - n_strided, seq, HEAD_DIM), BF16)] * 2
                       + [pltpu.VMEM((n_strided, seq, HEAD_DIM), F32)] * 3
                       + [pltpu.VMEM((ng, seq, HEAD_DIM), F32)] * 2
                       + [pltpu.VMEM((seq // QBLOCK, QBLOCK, kw_max), F32),
                          pltpu.VMEM((seq // QBLOCK, QBLOCK, kw_max), BF16),
                          pltpu.VMEM((seq // QBLOCK, QBLOCK, HEAD_DIM), F32)],
        compiler_params=_params(
            ("parallel", "parallel"),
            (3 * ng + 1) * _nbytes((seq, HEAD_DIM), BF16), len(tables) * slab,
            scratch_bytes=(ng - n_strided + 3 * n_strided + 2 * ng) * slab + 4 * slab),
        name="dilated",
    )(*([p] * (3 * ng)), gq, gk, *tables)


def _merge_kernel(oa_ref, ob_ref, oc_ref, ga_ref, gb_ref, gc_ref, wa_ref, wb_ref, wc_ref, o_ref):
    ya = jnp.dot(oa_ref[...], wa_ref[...], preferred_element_type=F32)
    yb = jnp.dot(ob_ref[...], wb_ref[...], preferred_element_type=F32)
    yc = jnp.dot(oc_ref[...], wc_ref[...], preferred_element_type=F32)
    merged = (ga_ref[...].astype(F32) * ya + gb_ref[...].astype(F32) * yb
              + gc_ref[...].astype(F32) * yc)
    o_ref[...] = merged.astype(o_ref.dtype)


def _merge(oa, ob, oc, p, w_oa, w_ob, w_oc, layer, tm, tn):
    m = oa.shape[0]
    d = w_oa.shape[2]
    nj = d // tn
    row = lambda a: pl.BlockSpec((tm, a.shape[1]), lambda i, j: (i, 0))
    wcol = lambda w: pl.BlockSpec((None, w.shape[1], tn), lambda i, j: (layer, 0, j))
    gate = lambda g: pl.BlockSpec((tm, tn), lambda i, j: (i, g * nj + j))
    return pl.pallas_call(
        _merge_kernel,
        grid=(m // tm, nj),
        in_specs=[row(oa), row(ob), row(oc), gate(0), gate(1), gate(2),
                  wcol(w_oa), wcol(w_ob), wcol(w_oc)],
        out_specs=pl.BlockSpec((tm, tn), lambda i, j: (i, j)),
        out_shape=jax.ShapeDtypeStruct((m, d), BF16),
        compiler_params=_params(
            ("parallel", "parallel"),
            _nbytes((tm, oa.shape[1] + ob.shape[1] + oc.shape[1]), BF16),
            4 * _nbytes((tm, tn), BF16),
            _nbytes((w_oa.shape[1] + w_ob.shape[1] + w_oc.shape[1], tn), BF16),
            scratch_bytes=4 * _nbytes((tm, tn), F32)),
        name="merge",
    )(oa, ob, oc, p, p, p, w_oa, w_ob, w_oc)


def _resid_matmul_kernel(x_ref, a_ref, w_ref, o_ref):
    o_ref[...] = x_ref[...] + jnp.dot(a_ref[...], w_ref[...], preferred_element_type=F32)


def _resid_matmul(x2d, a, w, layer, tm, tn):
    m, d = x2d.shape
    k = a.shape[1]
    return pl.pallas_call(
        _resid_matmul_kernel,
        grid=(m // tm, d // tn),
        in_specs=[
            pl.BlockSpec((tm, tn), lambda i, j: (i, j)),
            pl.BlockSpec((tm, k), lambda i, j: (i, 0)),
            pl.BlockSpec((None, k, tn), lambda i, j: (layer, 0, j)),
        ],
        out_specs=pl.BlockSpec((tm, tn), lambda i, j: (i, j)),
        out_shape=jax.ShapeDtypeStruct((m, d), F32),
        compiler_params=_params(
            ("parallel", "parallel"),
            2 * _nbytes((tm, tn), F32), _nbytes((tm, k), BF16), _nbytes((k, tn), BF16)),
        name="out_proj",
    )(x2d, a, w)


def _mlp_kernel(x_ref, g_ref, wu_ref, wd_ref, o_ref, xn_ref):
    c = pl.program_id(1)

    @pl.when(c == 0)
    def _():
        x = x_ref[...]
        ms = jnp.mean(x * x, axis=-1, keepdims=True)
        xn_ref[...] = (x * lax.rsqrt(ms + EPS) * g_ref[...]).astype(BF16)
        o_ref[...] = x

    h = jnp.dot(xn_ref[...], wu_ref[...], preferred_element_type=F32)
    h = jnp.square(jnp.maximum(h, 0.0)).astype(BF16)
    o_ref[...] += jnp.dot(h, wd_ref[...], preferred_element_type=F32)


def _mlp(x2d, gain, w_up, w_down, layer, tm, tf):
    m, d = x2d.shape
    f = w_up.shape[2]
    return pl.pallas_call(
        _mlp_kernel,
        grid=(m // tm, f // tf),
        in_specs=[
            pl.BlockSpec((tm, d), lambda i, c: (i, 0)),
            pl.BlockSpec((1, d), lambda i, c: (0, 0)),
            pl.BlockSpec((None, d, tf), lambda i, c: (layer, 0, c)),
            pl.BlockSpec((None, tf, d), lambda i, c: (layer, c, 0)),
        ],
        out_specs=pl.BlockSpec((tm, d), lambda i, c: (i, 0)),
        out_shape=jax.ShapeDtypeStruct((m, d), F32),
        scratch_shapes=[pltpu.VMEM((tm, d), BF16)],
        compiler_params=_params(
            ("parallel", "arbitrary"),
            2 * _nbytes((tm, d), F32), 2 * _nbytes((d, tf), BF16),
            scratch_bytes=_nbytes((tm, d), BF16) + _nbytes((tm, tf), F32)),
        name="mlp",
    )(x2d, gain, w_up, w_down)


def _largest_tile(n, cap):
    t = min(n, cap)
    while n % t:
        t //= 2
    return t


def kernel(x, attn_norm, w_in, b_gate, mla_q_lat_norm, w_uq, mla_kv_lat_norm, w_ukv,
           mla_q_head_norm, mla_k_head_norm, gqa_q_norm, gqa_k_norm, dil_q_norm, dil_k_norm,
           w_oa, w_ob, w_oc, w_out, mlp_norm, w_up, w_down):
    batch, seq, d = x.shape
    depth = w_in.shape[0]
    m = batch * seq
    d_ff = w_up.shape[2]
    gate_cols = 3 * d
    assert w_in.shape[2] == A_COLS + B_COLS + C_COLS + gate_cols
    assert seq % GRID_W == 0 and seq % (16 * QBLOCK) == 0 and d % LANES == 0
    tn_d = _largest_tile(d, 1024)

    tn_proj = 1792
    nq_b, nk_b = GQA_HEADS * HEAD_DIM, GQA_KV_HEADS * HEAD_DIM
    nh_c = DIL_HEADS * HEAD_DIM
    off_bq = 0
    off_bk = off_bq + nq_b
    off_bv = off_bk + nk_b
    off_cq = off_bv + nk_b
    off_ck = off_cq + nh_c
    off_cv = off_ck + nh_c
    off_aq = off_cv + nh_c
    off_akv = off_aq + MLA_Q_RANK
    off_ape = off_akv + MLA_KV_RANK
    n_proj = _round_up(off_ape + LANES, tn_proj)

    tm_big = _largest_tile(m, 1024)
    tm_seq = _largest_tile(seq, 1024)
    tm_mla = _largest_tile(seq, 512)

    pos = jnp.arange(seq, dtype=F32)
    row = jnp.repeat(jnp.arange(seq // GRID_W, dtype=F32), GRID_W)
    col = jnp.tile(jnp.arange(GRID_W, dtype=F32), seq // GRID_W)
    half_mla, half_ax, half_dil = MLA_ROPE // 2, HEAD_DIM // 4, PARTIAL_ROPE_DIM // 2
    rope_mla = _rope_tables(seq, ROPE_PAIR_SHIFT, [(0, half_mla, pos, MLA_ROPE_THETA)])
    rope_gqa = _rope_tables(seq, half_ax, [(0, half_ax, row, AXIAL_THETA),
                                           (2 * half_ax, half_ax, col, AXIAL_THETA)])
    rope_dil = _rope_tables(seq, half_dil, [(0, half_dil, pos, PARTIAL_ROPE_THETA)])

    def rope_slab(w):
        zeros = jnp.zeros(w.shape[:-1] + (ROPE_PAIR_SHIFT - half_mla,), w.dtype)
        return jnp.concatenate([w[..., :half_mla], zeros, w[..., half_mla:], zeros], axis=-1)

    w_oa_b, w_ob_b, w_oc_b, w_out_b = (w.astype(BF16) for w in (w_oa, w_ob, w_oc, w_out))
    w_up_b, w_down_b = w_up.astype(BF16), w_down.astype(BF16)

    x2d = x.reshape(m, d)
    for l in range(depth):
        wl = w_in[l]
        c1 = A_COLS + B_COLS + C_COLS
        w_p = jnp.concatenate([
            wl[:, A_COLS:c1],
            wl[:, :MLA_Q_RANK + MLA_KV_RANK],
            rope_slab(wl[:, MLA_Q_RANK + MLA_KV_RANK:A_COLS]),
            jnp.zeros((d, n_proj - off_ape - LANES), F32),
        ], axis=1).astype(BF16)
        w_g = wl[:, c1:].astype(BF16)
        gain = attn_norm[l][None, :]
        p = _proj(x2d, gain, w_p, jnp.zeros((1, n_proj), F32), False, tm_big, tn_proj)
        gates = _proj(x2d, gain, w_g, b_gate[l][None, :], True, tm_big,
                      _largest_tile(gate_cols, 2048))

        wq = w_uq[l].reshape(MLA_Q_RANK, MLA_HEADS, MLA_QK)
        wq = jnp.concatenate([wq[..., :MLA_NOPE], rope_slab(wq[..., MLA_NOPE:])], axis=-1
                             ).reshape(MLA_Q_RANK, MLA_HEADS * MLA_QK_PAD).astype(BF16)
        wkv = w_ukv[l].reshape(MLA_KV_RANK, MLA_HEADS, MLA_NOPE + MLA_V)
        wk = wkv[:, :, :MLA_NOPE].reshape(MLA_KV_RANK, MLA_HEADS * MLA_NOPE).astype(BF16)
        wv = wkv[:, :, MLA_NOPE:].reshape(MLA_KV_RANK, MLA_HEADS * MLA_V).astype(BF16)
        gqh = mla_q_head_norm[l] * (LOG2_E * MLA_QK ** -0.5)
        gqh = jnp.concatenate([gqh[:MLA_NOPE], rope_slab(gqh[MLA_NOPE:])])[None, :]
        gkn = mla_k_head_norm[l][None, :MLA_NOPE]
        gkr = rope_slab(mla_k_head_norm[l][MLA_NOPE:])[None, :]
        qa, ka, va = _mla_prep(p, off_aq, off_akv, off_ape, wq, wk, wv,
                               mla_q_lat_norm[l][None, :], mla_kv_lat_norm[l][None, :],
                               gqh, gkn, gkr, rope_mla[0], seq, tm_mla)
        oa = _attention(qa, ka, va, 0, 0, 0, MLA_HEADS, 1, MLA_QK_PAD, MLA_V, batch, seq)

        scale = HEAD_DIM ** -0.5
        gq = jnp.tile(gqa_q_norm[l] * (LOG2_E * scale), GQA_HEADS)[None, :]
        gk = jnp.tile(gqa_k_norm[l], GQA_KV_HEADS)[None, :]
        qb = _headnorm_rope(p, off_bq, nq_b, gq, rope_gqa, seq, tm_seq)
        kb = _headnorm_rope(p, off_bk, nk_b, gk, rope_gqa, seq, tm_seq)
        ob = _attention(qb, kb, p, 0, 0, off_bv, GQA_HEADS, GQA_GROUP, HEAD_DIM, HEAD_DIM,
                        batch, seq)

        oc = _dilated(p, off_cq, off_ck, off_cv, (dil_q_norm[l] * scale)[None, :],
                      dil_k_norm[l][None, :], rope_dil, batch, seq)

        merged = _merge(oa, ob, oc, gates, w_oa_b, w_ob_b, w_oc_b, l, tm_big, tn_d)
        x2d = _resid_matmul(x2d, merged, w_out_b, l, tm_big, tn_d)
        x2d = _mlp(x2d, mlp_norm[l][None, :], w_up_b, w_down_b, l,
                   _largest_tile(m, 1024), _largest_tile(d_ff, 512))
    return x2d.reshape(batch, seq, d)
```

```python
import functools

import numpy as np
import jax
import jax.numpy as jnp
from jax import lax
from jax.experimental import pallas as pl
from jax.experimental.pallas import tpu as pltpu

F32 = jnp.float32
BF16 = jnp.bfloat16

LANES = 128
VMEM_CAP_BYTES = 60000 * 1024

HEAD_DIM = 128
GRID_W = 64
EPS = 1e-6
NEG_INF = -1e30

MLA_HEADS = 8
MLA_Q_RANK = 512
MLA_KV_RANK = 256
MLA_NOPE = 128
MLA_ROPE = 64
MLA_V = 128
MLA_ROPE_THETA = 10000.0
MLA_QK = MLA_NOPE + MLA_ROPE
MLA_QK_PAD = 2 * LANES

GQA_HEADS = 8
GQA_KV_HEADS = 2
GQA_GROUP = GQA_HEADS // GQA_KV_HEADS
AXIAL_THETA = 10000.0

DIL_PATTERNS = ((128, 1), (512, 4), (2048, 16))
DIL_HPG = 4
DIL_HEADS = DIL_HPG * len(DIL_PATTERNS)
PARTIAL_ROPE_DIM = HEAD_DIM // 4
PARTIAL_ROPE_THETA = 500000.0
QBLOCK = 128
ATTN_SUB_ROWS = 256
LOG2_E = 1.4426950408889634
ATTN_HEADS_PER_STEP = 2
DIL_PREP_ROWS = 256

A_COLS = MLA_Q_RANK + MLA_KV_RANK + MLA_ROPE
B_COLS = (GQA_HEADS + 2 * GQA_KV_HEADS) * HEAD_DIM
C_COLS = 3 * DIL_HEADS * HEAD_DIM


def _round_up(n, m):
    return -(-n // m) * m


def _params(semantics, *block_bytes, scratch_bytes=0):
    need = 2 * sum(block_bytes) + scratch_bytes
    limit = min(VMEM_CAP_BYTES, need + (24 << 20))
    return pltpu.CompilerParams(dimension_semantics=semantics, vmem_limit_bytes=limit)


def _nbytes(shape, dtype):
    n = 1
    for s in shape:
        n *= s
    return n * jnp.dtype(dtype).itemsize


ROPE_PAIR_SHIFT = LANES // 2


def _rope_tables(seq, dist, segments):
    cos = np.ones((seq, LANES), np.float64)
    sin_x1 = np.zeros((seq, LANES), np.float64)
    sin_x2 = np.zeros((seq, LANES), np.float64)
    for first, half, pos, theta in segments:
        inv = theta ** (-np.arange(half, dtype=np.float64) / half)
        ang = pos[:, None] * inv[None, :]
        c, s = np.cos(ang), np.sin(ang)
        hi = first + dist
        cos[:, first:first + half] = c
        cos[:, hi:hi + half] = c
        sin_x1[:, first:first + half] = -s
        sin_x2[:, hi:hi + half] = s
    as_table = lambda t: jnp.asarray(t.astype(np.float32))
    if 2 * dist == LANES:
        return (as_table(cos), as_table(sin_x1 + sin_x2)), (dist,)
    return (as_table(cos), as_table(sin_x1), as_table(sin_x2)), (LANES - dist, dist)


def _rope(x, tables, shifts):
    out = x * tables[0]
    for table, shift in zip(tables[1:], shifts):
        out = out + pltpu.roll(x, shift, 1) * table
    return out


def _proj_kernel(x_ref, g_ref, w_ref, b_ref, o_ref, xn_ref, *, gate):
    @pl.when(pl.program_id(1) == 0)
    def _():
        x = x_ref[...]
        ms = jnp.mean(x * x, axis=-1, keepdims=True)
        xn_ref[...] = (x * lax.rsqrt(ms + EPS) * g_ref[...]).astype(BF16)

    y = jnp.dot(xn_ref[...], w_ref[...], preferred_element_type=F32) + b_ref[...]
    if gate:
        y = 1.0 / (1.0 + jnp.exp(-y))
    o_ref[...] = y.astype(o_ref.dtype)


def _proj(x2d, gain, w, bias, gate, tm, tn):
    m, d = x2d.shape
    n = w.shape[1]
    kern = functools.partial(_proj_kernel, gate=gate)
    return pl.pallas_call(
        kern,
        grid=(m // tm, n // tn),
        in_specs=[
            pl.BlockSpec((tm, d), lambda i, j: (i, 0)),
            pl.BlockSpec((1, d), lambda i, j: (0, 0)),
            pl.BlockSpec((d, tn), lambda i, j: (0, j)),
            pl.BlockSpec((1, tn), lambda i, j: (0, j)),
        ],
        out_specs=pl.BlockSpec((tm, tn), lambda i, j: (i, j)),
        out_shape=jax.ShapeDtypeStruct((m, n), BF16),
        scratch_shapes=[pltpu.VMEM((tm, d), BF16)],
        compiler_params=_params(
            ("parallel", "arbitrary"),
            _nbytes((tm, d), F32), _nbytes((d, tn), BF16), _nbytes((tm, tn), BF16),
            scratch_bytes=_nbytes((tm, d), BF16) + _nbytes((tm, d), F32)),
        name="proj",
    )(x2d, gain, w, bias)


def _row_meansq(x):
    n = x.shape[-1]
    assert n & (n - 1) == 0
    sq = x * x
    hi = sq.astype(BF16)
    lo = (sq - hi.astype(F32)).astype(BF16)
    weights = jnp.full((n, LANES), 1.0 / n, BF16)
    return (jnp.dot(hi, weights, preferred_element_type=F32)
            + jnp.dot(lo, weights, preferred_element_type=F32))


def _headnorm_rope_kernel(x_ref, g_ref, *rest, nheads, shifts):
    table_refs, o_ref = rest[:-1], rest[-1]
    tables = [t[...] for t in table_refs]
    for h in range(nheads):
        cols = slice(h * HEAD_DIM, (h + 1) * HEAD_DIM)
        x = x_ref[:, cols].astype(F32)
        ms = _row_meansq(x)
        y = x * lax.rsqrt(ms + EPS) * g_ref[:, cols]
        o_ref[:, cols] = _rope(y, tables, shifts).astype(o_ref.dtype)


def _headnorm_rope(p, col_off, width, gains, rope, seq, tm):
    m = p.shape[0]
    assert col_off % width == 0
    tables, shifts = rope
    cb = col_off // width
    sb = seq // tm
    kern = functools.partial(_headnorm_rope_kernel, nheads=width // HEAD_DIM, shifts=shifts)
    tab_spec = pl.BlockSpec((tm, LANES), lambda i: (i % sb, 0))
    return pl.pallas_call(
        kern,
        grid=(m // tm,),
        in_specs=[
            pl.BlockSpec((tm, width), lambda i: (i, cb)),
            pl.BlockSpec((1, width), lambda i: (0, 0)),
        ] + [tab_spec] * len(tables),
        out_specs=pl.BlockSpec((tm, width), lambda i: (i, 0)),
        out_shape=jax.ShapeDtypeStruct((m, width), BF16),
        compiler_params=_params(
            ("parallel",),
            2 * _nbytes((tm, width), BF16), len(tables) * _nbytes((tm, LANES), F32),
            scratch_bytes=4 * _nbytes((tm, LANES), F32)),
        name="headnorm_rope",
    )(p, gains, *tables)


def _mla_prep_kernel(cq_ref, ckv_ref, kpe_ref, wq_ref, wk_ref, wv_ref, gql_ref, gkl_ref,
                     gqh_ref, gkn_ref, gkr_ref, cos_ref, sin_ref,
                     q_ref, k_ref, v_ref):
    tables, shifts = (cos_ref[...], sin_ref[...]), (ROPE_PAIR_SHIFT,)
    sumsq = lambda t: jnp.sum(t * t, axis=-1, keepdims=True)

    cq = cq_ref[...].astype(F32)
    cqn = cq * lax.rsqrt(sumsq(cq) * (1.0 / MLA_Q_RANK) + EPS) * gql_ref[...]
    q = jnp.dot(cqn.astype(BF16), wq_ref[...], preferred_element_type=F32)
    gqh = gqh_ref[...]
    for h in range(MLA_HEADS):
        nope = slice(h * MLA_QK_PAD, h * MLA_QK_PAD + LANES)
        rope = slice(h * MLA_QK_PAD + LANES, (h + 1) * MLA_QK_PAD)
        inv = lax.rsqrt(sumsq(q[:, h * MLA_QK_PAD:(h + 1) * MLA_QK_PAD]) * (1.0 / MLA_QK) + EPS)
        q_ref[:, nope] = (q[:, nope] * inv * gqh[:, :LANES]).astype(q_ref.dtype)
        q_ref[:, rope] = _rope(q[:, rope] * inv * gqh[:, LANES:], tables, shifts).astype(q_ref.dtype)

    ckv = ckv_ref[...].astype(F32)
    ckvn = (ckv * lax.rsqrt(sumsq(ckv) * (1.0 / MLA_KV_RANK) + EPS) * gkl_ref[...]).astype(BF16)
    kn = jnp.dot(ckvn, wk_ref[...], preferred_element_type=F32)
    v_ref[...] = jnp.dot(ckvn, wv_ref[...], preferred_element_type=F32).astype(v_ref.dtype)
    kpe = kpe_ref[...].astype(F32)
    pe_sq = sumsq(kpe)
    gkn = gkn_ref[...]
    kpe_rot = _rope(kpe * gkr_ref[...], tables, shifts)
    for h in range(MLA_HEADS):
        kh = kn[:, h * MLA_NOPE:(h + 1) * MLA_NOPE]
        inv = lax.rsqrt((sumsq(kh) + pe_sq) * (1.0 / MLA_QK) + EPS)
        k_ref[:, h * MLA_QK_PAD:h * MLA_QK_PAD + LANES] = (kh * inv * gkn).astype(k_ref.dtype)
        k_ref[:, h * MLA_QK_PAD + LANES:(h + 1) * MLA_QK_PAD] = (kpe_rot * inv).astype(k_ref.dtype)


def _mla_prep(p, off_cq, off_ckv, off_kpe, wq, wk, wv, gql, gkl, gqh, gkn, gkr, tables, seq, tm):
    m = p.shape[0]
    sb = seq // tm
    const = lambda shape: pl.BlockSpec(shape, lambda i: (0, 0))
    tab_spec = pl.BlockSpec((tm, LANES), lambda i: (i % sb, 0))
    qk_cols = MLA_HEADS * MLA_QK_PAD
    v_cols = MLA_HEADS * MLA_V
    return pl.pallas_call(
        _mla_prep_kernel,
        grid=(m // tm,),
        in_specs=[
            pl.BlockSpec((tm, MLA_Q_RANK), lambda i: (i, off_cq // MLA_Q_RANK)),
            pl.BlockSpec((tm, MLA_KV_RANK), lambda i: (i, off_ckv // MLA_KV_RANK)),
            pl.BlockSpec((tm, LANES), lambda i: (i, off_kpe // LANES)),
            const(wq.shape), const(wk.shape), const(wv.shape),
            const(gql.shape), const(gkl.shape), const(gqh.shape), const(gkn.shape), const(gkr.shape),
            tab_spec, tab_spec,
        ],
        out_specs=[
            pl.BlockSpec((tm, qk_cols), lambda i: (i, 0)),
            pl.BlockSpec((tm, qk_cols), lambda i: (i, 0)),
            pl.BlockSpec((tm, v_cols), lambda i: (i, 0)),
        ],
        out_shape=[
            jax.ShapeDtypeStruct((m, qk_cols), BF16),
            jax.ShapeDtypeStruct((m, qk_cols), BF16),
            jax.ShapeDtypeStruct((m, v_cols), BF16),
        ],
        compiler_params=_params(
            ("parallel",),
            _nbytes((tm, MLA_Q_RANK + MLA_KV_RANK + LANES), BF16),
            _nbytes(wq.shape, BF16), _nbytes(wk.shape, BF16), _nbytes(wv.shape, BF16),
            _nbytes((tm, 2 * qk_cols + v_cols), BF16), 2 * _nbytes((tm, LANES), F32),
            scratch_bytes=3 * _nbytes((tm, qk_cols), F32)),
        name="mla_prep",
    )(p, p, p, wq, wk, wv, gql, gkl, gqh, gkn, gkr, *tables)


def _attn_kernel(q_ref, k_ref, v_ref, o_ref, *, sub, heads, dk, dv, shared_kv):
    for h in range(heads):
        kv = 0 if shared_kv else h
        k = k_ref[:, kv * dk:(kv + 1) * dk]
        v = v_ref[:, kv * dv:(kv + 1) * dv]
        v_ones = jnp.concatenate([v, jnp.ones_like(v)], axis=1)
        for r0 in range(0, q_ref.shape[0], sub):
            rows = slice(r0, r0 + sub)
            s = lax.dot_general(q_ref[rows, h * dk:(h + 1) * dk], k, (((1,), (1,)), ((), ())),
                                preferred_element_type=F32)
            p = jnp.exp2(s - jnp.max(s, axis=-1, keepdims=True)).astype(BF16)
            ov = jnp.dot(p, v_ones, preferred_element_type=F32)
            o_ref[rows, h * dv:(h + 1) * dv] = (ov[:, :dv] * (1.0 / ov[:, dv:])).astype(o_ref.dtype)


def _attention(q_arr, k_arr, v_arr, q_off, k_off, v_off, nheads, group, dk, dv, batch, seq):
    hps = ATTN_HEADS_PER_STEP
    shared_kv = group % hps == 0
    assert nheads % hps == 0 and (shared_kv or group == 1)
    kv_w = 1 if shared_kv else hps
    assert q_off % (hps * dk) == 0 and k_off % (kv_w * dk) == 0 and v_off % (kv_w * dv) == 0
    qb, kb, vb = q_off // (hps * dk), k_off // (kv_w * dk), v_off // (kv_w * dv)
    kv_of = (lambda j: (j * hps) // group) if shared_kv else (lambda j: j)
    sub = min(seq, ATTN_SUB_ROWS)
    kern = functools.partial(_attn_kernel, sub=sub, heads=hps, dk=dk, dv=dv, shared_kv=shared_kv)
    return pl.pallas_call(
        kern,
        grid=(batch, nheads // hps),
        in_specs=[
            pl.BlockSpec((seq, hps * dk), lambda b, j: (b, qb + j)),
            pl.BlockSpec((seq, kv_w * dk), lambda b, j: (b, kb + kv_of(j))),
            pl.BlockSpec((seq, kv_w * dv), lambda b, j: (b, vb + kv_of(j))),
        ],
        out_specs=pl.BlockSpec((seq, hps * dv), lambda b, j: (b, j)),
        out_shape=jax.ShapeDtypeStruct((batch * seq, nheads * dv), BF16),
        compiler_params=_params(
            ("parallel", "parallel"),
            _nbytes((seq, hps * dk), BF16), _nbytes((seq, kv_w * dk), BF16),
            _nbytes((seq, kv_w * dv), BF16), _nbytes((seq, hps * dv), BF16),
            scratch_bytes=4 * _nbytes((sub, seq), F32)),
        name="attention",
    )(q_arr, k_arr, v_arr)


def _dilated_kernel(q0_ref, q1_ref, q2_ref, k0_ref, k1_ref, k2_ref, v0_ref, v1_ref, v2_ref,
                    gq_ref, gk_ref, *rest, seq, patterns, shifts):
    n_tab = len(shifts) + 1
    table_refs, out_ref = rest[:n_tab], rest[n_tab]
    qb_ref, kb_ref, qf_ref, kf_ref, vf_ref, of_ref, lf_ref, s_ref, p_ref, m_ref = rest[n_tab + 1:]
    q_refs, k_refs, v_refs = (q0_ref, q1_ref, q2_ref), (k0_ref, k1_ref, k2_ref), (v0_ref, v1_ref, v2_ref)
    strided = [g for g, (_, dil) in enumerate(patterns) if dil > 1]
    dense = [g for g, (_, dil) in enumerate(patterns) if dil == 1]

    def prepared(src, gain, rows):
        x = src[rows, :].astype(F32)
        ms = _row_meansq(x)
        return _rope(x * lax.rsqrt(ms + EPS) * gain[...], [t[rows, :] for t in table_refs], shifts)

    for g, (window, dil) in enumerate(patterns):
        radius = window // (2 * dil)
        length = seq // dil
        kw = min(length, QBLOCK + 2 * radius)
        slot = strided.index(g) if dil > 1 else dense.index(g)
        for r0 in range(0, seq, DIL_PREP_ROWS):
            chunk = slice(r0, r0 + DIL_PREP_ROWS)
            if dil > 1:
                qf_ref[slot, chunk, :] = prepared(q_refs[g], gq_ref, chunk)
                kf_ref[slot, chunk, :] = prepared(k_refs[g], gk_ref, chunk)
                vf_ref[slot, chunk, :] = v_refs[g][chunk, :].astype(F32)
            else:
                qb_ref[slot, chunk, :] = prepared(q_refs[g], gq_ref, chunk).astype(BF16)
                kb_ref[slot, chunk, :] = prepared(k_refs[g], gk_ref, chunk).astype(BF16)
        def rows(r, start, n, dil=dil):
            return pl.ds(start, n) if dil == 1 else pl.ds(start * dil + r, n, stride=dil)

        def window_of(dense_ref, copy, r, start, n, dil=dil, slot=slot, rows=rows):
            if dil == 1:
                return dense_ref[rows(r, start, n), :]
            return copy[slot, rows(r, start, n), :].astype(BF16)

        tiles = [(r, t * QBLOCK, min(max(t * QBLOCK - radius, 0), length - kw))
                 for r in range(dil) for t in range(length // QBLOCK)]
        for i, (r, q0, ks) in enumerate(tiles):
            q = window_of(qb_ref.at[slot] if dil == 1 else None, qf_ref, r, q0, QBLOCK)
            k = window_of(kb_ref.at[slot] if dil == 1 else None, kf_ref, r, ks, kw)
            s = lax.dot_general(q, k, (((1,), (1,)), ((), ())), preferred_element_type=F32)
            rel = (lax.broadcasted_iota(jnp.int32, s.shape, 0)
                   - lax.broadcasted_iota(jnp.int32, s.shape, 1)) + (q0 - ks)
            s_ref[i, :, :kw] = jnp.where(jnp.abs(rel) <= radius, s, NEG_INF)
        for i in range(len(tiles)):
            s = s_ref[i, :, :kw]
            m = jnp.max(s, axis=-1, keepdims=True)
            p_ref[i, :, :kw] = jnp.exp(s - m).astype(BF16)
            m_ref[i] = jnp.broadcast_to(m, (QBLOCK, HEAD_DIM))
        for i, (r, q0, ks) in enumerate(tiles):
            v = window_of(v_refs[g], vf_ref, r, ks, kw)
            ov = jnp.dot(p_ref[i, :, :kw], jnp.concatenate([v, jnp.ones_like(v)], axis=1),
                         preferred_element_type=F32)
            l = ov[:, HEAD_DIM:]
            of_ref[g, rows(r, q0, QBLOCK), :] = ov[:, :HEAD_DIM] * (1.0 / l)
            lf_ref[g, rows(r, q0, QBLOCK), :] = m_ref[i] + jnp.log(l)

    la, lb, lc = lf_ref[0], lf_ref[1], lf_ref[2]
    m = jnp.maximum(jnp.maximum(la, lb), lc)
    ea, eb, ec = jnp.exp(la - m), jnp.exp(lb - m), jnp.exp(lc - m)
    num = ea * of_ref[0] + eb * of_ref[1] + ec * of_ref[2]
    out_ref[...] = (num * (1.0 / (ea + eb + ec))).astype(out_ref.dtype)


def _dilated(p, off_q, off_k, off_v, gq, gk, rope, batch, seq):
    ng = len(DIL_PATTERNS)
    n_strided = sum(dil > 1 for _, dil in DIL_PATTERNS)
    tables, shifts = rope
    kw_max = max(min(seq // dil, QBLOCK + window // dil) for window, dil in DIL_PATTERNS)
    kern = functools.partial(_dilated_kernel, seq=seq, patterns=DIL_PATTERNS, shifts=shifts)
    group_w = DIL_HPG * HEAD_DIM

    def head(off, g):
        assert off % HEAD_DIM == 0
        cb = (off + g * group_w) // HEAD_DIM
        return pl.BlockSpec((seq, HEAD_DIM), lambda b, h: (b, cb + h))

    const = lambda shape: pl.BlockSpec(shape, lambda b, h: (0, 0))
    slab = _nbytes((seq, HEAD_DIM), F32)
    return pl.pallas_call(
        kern,
        grid=(batch, DIL_HPG),
        in_specs=([head(off_q, g) for g in range(ng)] + [head(off_k, g) for g in range(ng)]
                  + [head(off_v, g) for g in range(ng)]
                  + [const((1, HEAD_DIM))] * 2 + [const((seq, LANES))] * len(tables)),
        out_specs=pl.BlockSpec((seq, HEAD_DIM), lambda b, h: (b, h)),
        out_shape=jax.ShapeDtypeStruct((batch * seq, group_w), BF16),
        scratch_shapes=[pltpu.VMEM((ng - n_strided, seq, HEAD_DIM), BF16)] * 2
                       + [pltpu.VMEM((n_strided, seq, HEAD_DIM), F32)] * 3
                       + [pltpu.VMEM((ng, seq, HEAD_DIM), F32)] * 2
                       + [pltpu.VMEM((seq // QBLOCK, QBLOCK, kw_max), F32),
                          pltpu.VMEM((seq // QBLOCK, QBLOCK, kw_max), BF16),
                          pltpu.VMEM((seq // QBLOCK, QBLOCK, HEAD_DIM), F32)],
        compiler_params=_params(
            ("parallel", "parallel"),
            (3 * ng + 1) * _nbytes((seq, HEAD_DIM), BF16), len(tables) * slab,
            scratch_bytes=(ng - n_strided + 3 * n_strided + 2 * ng) * slab + 4 * slab),
        name="dilated",
    )(*([p] * (3 * ng)), gq, gk, *tables)


def _merge_kernel(oa_ref, ob_ref, oc_ref, ga_ref, gb_ref, gc_ref, wa_ref, wb_ref, wc_ref, o_ref):
    ya = jnp.dot(oa_ref[...], wa_ref[...], preferred_element_type=F32)
    yb = jnp.dot(ob_ref[...], wb_ref[...], preferred_element_type=F32)
    yc = jnp.dot(oc_ref[...], wc_ref[...], preferred_element_type=F32)
    merged = (ga_ref[...].astype(F32) * ya + gb_ref[...].astype(F32) * yb
              + gc_ref[...].astype(F32) * yc)
    o_ref[...] = merged.astype(o_ref.dtype)


def _merge(oa, ob, oc, p, w_oa, w_ob, w_oc, layer, tm, tn):
    m = oa.shape[0]
    d = w_oa.shape[2]
    nj = d // tn
    row = lambda a: pl.BlockSpec((tm, a.shape[1]), lambda i, j: (i, 0))
    wcol = lambda w: pl.BlockSpec((None, w.shape[1], tn), lambda i, j: (layer, 0, j))
    gate = lambda g: pl.BlockSpec((tm, tn), lambda i, j: (i, g * nj + j))
    return pl.pallas_call(
        _merge_kernel,
        grid=(m // tm, nj),
        in_specs=[row(oa), row(ob), row(oc), gate(0), gate(1), gate(2),
                  wcol(w_oa), wcol(w_ob), wcol(w_oc)],
        out_specs=pl.BlockSpec((tm, tn), lambda i, j: (i, j)),
        out_shape=jax.ShapeDtypeStruct((m, d), BF16),
        compiler_params=_params(
            ("parallel", "parallel"),
            _nbytes((tm, oa.shape[1] + ob.shape[1] + oc.shape[1]), BF16),
            4 * _nbytes((tm, tn), BF16),
            _nbytes((w_oa.shape[1] + w_ob.shape[1] + w_oc.shape[1], tn), BF16),
            scratch_bytes=4 * _nbytes((tm, tn), F32)),
        name="merge",
    )(oa, ob, oc, p, p, p, w_oa, w_ob, w_oc)


def _resid_matmul_kernel(x_ref, a_ref, w_ref, o_ref):
    o_ref[...] = x_ref[...] + jnp.dot(a_ref[...], w_ref[...], preferred_element_type=F32)


def _resid_matmul(x2d, a, w, layer, tm, tn):
    m, d = x2d.shape
    k = a.shape[1]
    return pl.pallas_call(
        _resid_matmul_kernel,
        grid=(m // tm, d // tn),
        in_specs=[
            pl.BlockSpec((tm, tn), lambda i, j: (i, j)),
            pl.BlockSpec((tm, k), lambda i, j: (i, 0)),
            pl.BlockSpec((None, k, tn), lambda i, j: (layer, 0, j)),
        ],
        out_specs=pl.BlockSpec((tm, tn), lambda i, j: (i, j)),
        out_shape=jax.ShapeDtypeStruct((m, d), F32),
        compiler_params=_params(
            ("parallel", "parallel"),
            2 * _nbytes((tm, tn), F32), _nbytes((tm, k), BF16), _nbytes((k, tn), BF16)),
        name="out_proj",
    )(x2d, a, w)


def _mlp_kernel(x_ref, g_ref, wu_ref, wd_ref, o_ref, xn_ref):
    c = pl.program_id(1)

    @pl.when(c == 0)
    def _():
        x = x_ref[...]
        ms = jnp.mean(x * x, axis=-1, keepdims=True)
        xn_ref[...] = (x * lax.rsqrt(ms + EPS) * g_ref[...]).astype(BF16)
        o_ref[...] = x

    h = jnp.dot(xn_ref[...], wu_ref[...], preferred_element_type=F32)
    h = jnp.square(jnp.maximum(h, 0.0)).astype(BF16)
    o_ref[...] += jnp.dot(h, wd_ref[...], preferred_element_type=F32)


def _mlp(x2d, gain, w_up, w_down, layer, tm, tf):
    m, d = x2d.shape
    f = w_up.shape[2]
    return pl.pallas_call(
        _mlp_kernel,
        grid=(m // tm, f // tf),
        in_specs=[
            pl.BlockSpec((tm, d), lambda i, c: (i, 0)),
            pl.BlockSpec((1, d), lambda i, c: (0, 0)),
            pl.BlockSpec((None, d, tf), lambda i, c: (layer, 0, c)),
            pl.BlockSpec((None, tf, d), lambda i, c: (layer, c, 0)),
        ],
        out_specs=pl.BlockSpec((tm, d), lambda i, c: (i, 0)),
        out_shape=jax.ShapeDtypeStruct((m, d), F32),
        scratch_shapes=[pltpu.VMEM((tm, d), BF16)],
        compiler_params=_params(
            ("parallel", "arbitrary"),
            2 * _nbytes((tm, d), F32), 2 * _nbytes((d, tf), BF16),
            scratch_bytes=_nbytes((tm, d), BF16) + _nbytes((tm, tf), F32)),
        name="mlp",
    )(x2d, gain, w_up, w_down)


def _largest_tile(n, cap):
    t = min(n, cap)
    while n % t:
        t //= 2
    return t


def kernel(x, attn_norm, w_in, b_gate, mla_q_lat_norm, w_uq, mla_kv_lat_norm, w_ukv,
           mla_q_head_norm, mla_k_head_norm, gqa_q_norm, gqa_k_norm, dil_q_norm, dil_k_norm,
           w_oa, w_ob, w_oc, w_out, mlp_norm, w_up, w_down):
    batch, seq, d = x.shape
    depth = w_in.shape[0]
    m = batch * seq
    d_ff = w_up.shape[2]
    gate_cols = 3 * d
    assert w_in.shape[2] == A_COLS + B_COLS + C_COLS + gate_cols
    assert seq % GRID_W == 0 and seq % (16 * QBLOCK) == 0 and d % LANES == 0

    tn_proj = 1792
    nq_b, nk_b = GQA_HEADS * HEAD_DIM, GQA_KV_HEADS * HEAD_DIM
    nh_c = DIL_HEADS * HEAD_DIM
    off_bq = 0
    off_bk = off_bq + nq_b
    off_bv = off_bk + nk_b
    off_cq = off_bv + nk_b
    off_ck = off_cq + nh_c
    off_cv = off_ck + nh_c
    off_aq = off_cv + nh_c
    off_akv = off_aq + MLA_Q_RANK
    off_ape = off_akv + MLA_KV_RANK
    n_proj = _round_up(off_ape + LANES, tn_proj)

    tm_big = _largest_tile(m, 1024)
    tm_half = _largest_tile(m, 512)
    tm_seq = _largest_tile(seq, 1024)
    tm_mla = _largest_tile(seq, 512)

    pos = np.arange(seq, dtype=np.float64)
    row = np.repeat(np.arange(seq // GRID_W, dtype=np.float64), GRID_W)
    col = np.tile(np.arange(GRID_W, dtype=np.float64), seq // GRID_W)
    half_mla, half_ax, half_dil = MLA_ROPE // 2, HEAD_DIM // 4, PARTIAL_ROPE_DIM // 2
    rope_mla = _rope_tables(seq, ROPE_PAIR_SHIFT, [(0, half_mla, pos, MLA_ROPE_THETA)])
    rope_gqa = _rope_tables(seq, half_ax, [(0, half_ax, row, AXIAL_THETA),
                                           (2 * half_ax, half_ax, col, AXIAL_THETA)])
    rope_dil = _rope_tables(seq, half_dil, [(0, half_dil, pos, PARTIAL_ROPE_THETA)])

    def rope_slab(w):
        zeros = jnp.zeros(w.shape[:-1] + (ROPE_PAIR_SHIFT - half_mla,), w.dtype)
        return jnp.concatenate([w[..., :half_mla], zeros, w[..., half_mla:], zeros], axis=-1)

    w_oa_b, w_ob_b, w_oc_b, w_out_b = (w.astype(BF16) for w in (w_oa, w_ob, w_oc, w_out))
    w_up_b, w_down_b = w_up.astype(BF16), w_down.astype(BF16)

    x2d = x.reshape(m, d)
    for l in range(depth):
        wl = w_in[l]
        c1 = A_COLS + B_COLS + C_COLS
        w_p = jnp.concatenate([
            wl[:, A_COLS:c1],
            wl[:, :MLA_Q_RANK + MLA_KV_RANK],
            rope_slab(wl[:, MLA_Q_RANK + MLA_KV_RANK:A_COLS]),
            jnp.zeros((d, n_proj - off_ape - LANES), F32),
        ], axis=1).astype(BF16)
        w_g = wl[:, c1:].astype(BF16)
        gain = attn_norm[l][None, :]
        p = _proj(x2d, gain, w_p, jnp.zeros((1, n_proj), F32), False, tm_big, tn_proj)
        gates = _proj(x2d, gain, w_g, b_gate[l][None, :], True, tm_big,
                      _largest_tile(gate_cols, 2048))

        wq = w_uq[l].reshape(MLA_Q_RANK, MLA_HEADS, MLA_QK)
        wq = jnp.concatenate([wq[..., :MLA_NOPE], rope_slab(wq[..., MLA_NOPE:])], axis=-1
                             ).reshape(MLA_Q_RANK, MLA_HEADS * MLA_QK_PAD).astype(BF16)
        wkv = w_ukv[l].reshape(MLA_KV_RANK, MLA_HEADS, MLA_NOPE + MLA_V)
        wk = wkv[:, :, :MLA_NOPE].reshape(MLA_KV_RANK, MLA_HEADS * MLA_NOPE).astype(BF16)
        wv = wkv[:, :, MLA_NOPE:].reshape(MLA_KV_RANK, MLA_HEADS * MLA_V).astype(BF16)
        gqh = mla_q_head_norm[l] * (LOG2_E * MLA_QK ** -0.5)
        gqh = jnp.concatenate([gqh[:MLA_NOPE], rope_slab(gqh[MLA_NOPE:])])[None, :]
        gkn = mla_k_head_norm[l][None, :MLA_NOPE]
        gkr = rope_slab(mla_k_head_norm[l][MLA_NOPE:])[None, :]
        qa, ka, va = _mla_prep(p, off_aq, off_akv, off_ape, wq, wk, wv,
                               mla_q_lat_norm[l][None, :], mla_kv_lat_norm[l][None, :],
                               gqh, gkn, gkr, rope_mla[0], seq, tm_mla)
        oa = _attention(qa, ka, va, 0, 0, 0, MLA_HEADS, 1, MLA_QK_PAD, MLA_V, batch, seq)

        scale = HEAD_DIM ** -0.5
        gq = jnp.tile(gqa_q_norm[l] * (LOG2_E * scale), GQA_HEADS)[None, :]
        gk = jnp.tile(gqa_k_norm[l], GQA_KV_HEADS)[None, :]
        qb = _headnorm_rope(p, off_bq, nq_b, gq, rope_gqa, seq, tm_seq)
        kb = _headnorm_rope(p, off_bk, nk_b, gk, rope_gqa, seq, tm_seq)
        ob = _attention(qb, kb, p, 0, 0, off_bv, GQA_HEADS, GQA_GROUP, HEAD_DIM, HEAD_DIM,
                        batch, seq)

        oc = _dilated(p, off_cq, off_ck, off_cv, (dil_q_norm[l] * scale)[None, :],
                      dil_k_norm[l][None, :], rope_dil, batch, seq)

        merged = _merge(oa, ob, oc, gates, w_oa_b, w_ob_b, w_oc_b, l, tm_half, d)
        x2d = _resid_matmul(x2d, merged, w_out_b, l, tm_half, d)
        x2d = _mlp(x2d, mlp_norm[l][None, :], w_up_b, w_down_b, l,
                   _largest_tile(m, 1024), _largest_tile(d_ff, 512))
    return x2d.reshape(batch, seq, d)
```

```python
import functools

import numpy as np
import jax
import jax.numpy as jnp
from jax import lax
from jax.experimental import pallas as pl
from jax.experimental.pallas import tpu as pltpu

F32 = jnp.float32
BF16 = jnp.bfloat16

LANES = 128
VMEM_CAP_BYTES = 60000 * 1024

HEAD_DIM = 128
GRID_W = 64
EPS = 1e-6
NEG_INF = -1e30

MLA_HEADS = 8
MLA_Q_RANK = 512
MLA_KV_RANK = 256
MLA_NOPE = 128
MLA_ROPE = 64
MLA_V = 128
MLA_ROPE_THETA = 10000.0
MLA_QK = MLA_NOPE + MLA_ROPE
MLA_QK_PAD = 2 * LANES

GQA_HEADS = 8
GQA_KV_HEADS = 2
GQA_GROUP = GQA_HEADS // GQA_KV_HEADS
AXIAL_THETA = 10000.0

DIL_PATTERNS = ((128, 1), (512, 4), (2048, 16))
DIL_HPG = 4
DIL_HEADS = DIL_HPG * len(DIL_PATTERNS)
PARTIAL_ROPE_DIM = HEAD_DIM // 4
PARTIAL_ROPE_THETA = 500000.0
QBLOCK = 128
ATTN_SUB_ROWS = 256
LOG2_E = 1.4426950408889634
ATTN_HEADS_PER_STEP = 2
DIL_PREP_ROWS = 256

A_COLS = MLA_Q_RANK + MLA_KV_RANK + MLA_ROPE
B_COLS = (GQA_HEADS + 2 * GQA_KV_HEADS) * HEAD_DIM
C_COLS = 3 * DIL_HEADS * HEAD_DIM


def _round_up(n, m):
    return -(-n // m) * m


def _params(semantics, *block_bytes, scratch_bytes=0):
    need = 2 * sum(block_bytes) + scratch_bytes
    limit = min(VMEM_CAP_BYTES, need + (24 << 20))
    return pltpu.CompilerParams(dimension_semantics=semantics, vmem_limit_bytes=limit)


def _nbytes(shape, dtype):
    n = 1
    for s in shape:
        n *= s
    return n * jnp.dtype(dtype).itemsize


def _rope_tables(seq, dist, segments):
    cos = np.ones((seq, LANES), np.float64)
    sin_x1 = np.zeros((seq, LANES), np.float64)
    sin_x2 = np.zeros((seq, LANES), np.float64)
    for first, half, pos, theta in segments:
        inv = theta ** (-np.arange(half, dtype=np.float64) / half)
        ang = pos[:, None] * inv[None, :]
        c, s = np.cos(ang), np.sin(ang)
        hi = first + dist
        cos[:, first:first + half] = c
        cos[:, hi:hi + half] = c
        sin_x1[:, first:first + half] = -s
        sin_x2[:, hi:hi + half] = s
    as_table = lambda t: jnp.asarray(t.astype(np.float32))
    if 2 * dist == LANES:
        return (as_table(cos), as_table(sin_x1 + sin_x2)), (dist,)
    return (as_table(cos), as_table(sin_x1), as_table(sin_x2)), (LANES - dist, dist)


def _rope(x, tables, shifts):
    out = x * tables[0]
    for table, shift in zip(tables[1:], shifts):
        out = out + pltpu.roll(x, shift, 1) * table
    return out


def _proj_kernel(x_ref, g_ref, w_ref, o_ref, xn_ref):
    @pl.when(pl.program_id(1) == 0)
    def _():
        x = x_ref[...]
        ms = jnp.mean(x * x, axis=-1, keepdims=True)
        xn_ref[...] = (x * lax.rsqrt(ms + EPS) * g_ref[...]).astype(BF16)

    o_ref[...] = jnp.dot(xn_ref[...], w_ref[...], preferred_element_type=F32).astype(o_ref.dtype)


def _proj(x2d, gain, w, tm, tn):
    m, d = x2d.shape
    n = w.shape[1]
    return pl.pallas_call(
        _proj_kernel,
        grid=(m // tm, n // tn),
        in_specs=[
            pl.BlockSpec((tm, d), lambda i, j: (i, 0)),
            pl.BlockSpec((1, d), lambda i, j: (0, 0)),
            pl.BlockSpec((d, tn), lambda i, j: (0, j)),
        ],
        out_specs=[pl.BlockSpec((tm, tn), lambda i, j: (i, j)),
                   pl.BlockSpec((tm, d), lambda i, j: (i, 0))],
        out_shape=[jax.ShapeDtypeStruct((m, n), BF16), jax.ShapeDtypeStruct((m, d), BF16)],
        compiler_params=_params(
            ("parallel", "arbitrary"),
            _nbytes((tm, d), F32), _nbytes((d, tn), BF16), _nbytes((tm, tn), BF16),
            _nbytes((tm, d), BF16), scratch_bytes=_nbytes((tm, d), F32)),
        name="proj",
    )(x2d, gain, w)


def _w_in_kernel(w_ref, wp_ref, wg_ref, *, a_cols, head_cols):
    rows = w_ref.shape[0]
    a_pad = _round_up(a_cols, LANES)
    wp_ref[:, :head_cols] = w_ref[:, a_cols:a_cols + head_cols].astype(BF16)
    lane = lax.broadcasted_iota(jnp.int32, (rows, a_pad), 1)
    wp_ref[:, head_cols:head_cols + a_pad] = jnp.where(
        lane < a_cols, w_ref[:, :a_pad], 0.0).astype(BF16)
    tail = wp_ref.shape[1] - head_cols - a_pad
    if tail:
        wp_ref[:, head_cols + a_pad:] = jnp.zeros((rows, tail), BF16)
    wg_ref[...] = w_ref[:, a_cols + head_cols:].astype(BF16)


def _split_w_in(w_in, layer, a_cols, head_cols, n_proj, tk):
    _, d, n = w_in.shape
    gate_cols = n - a_cols - head_cols
    kern = functools.partial(_w_in_kernel, a_cols=a_cols, head_cols=head_cols)
    return pl.pallas_call(
        kern,
        grid=(d // tk,),
        in_specs=[pl.BlockSpec((None, tk, n), lambda i: (layer, i, 0))],
        out_specs=[pl.BlockSpec((tk, n_proj), lambda i: (i, 0)),
                   pl.BlockSpec((tk, gate_cols), lambda i: (i, 0))],
        out_shape=[jax.ShapeDtypeStruct((d, n_proj), BF16),
                   jax.ShapeDtypeStruct((d, gate_cols), BF16)],
        compiler_params=_params(("parallel",), _nbytes((tk, n), F32),
                                _nbytes((tk, n_proj + gate_cols), BF16)),
        name="split_w_in",
    )(w_in)


def _row_meansq(x):
    n = x.shape[-1]
    assert n & (n - 1) == 0
    sq = x * x
    hi = sq.astype(BF16)
    lo = (sq - hi.astype(F32)).astype(BF16)
    weights = jnp.full((n, LANES), 1.0 / n, BF16)
    return (jnp.dot(hi, weights, preferred_element_type=F32)
            + jnp.dot(lo, weights, preferred_element_type=F32))


def _headnorm_rope_kernel(x_ref, g_ref, *rest, nheads, shifts):
    table_refs, o_ref = rest[:-1], rest[-1]
    tables = [t[...] for t in table_refs]
    for h in range(nheads):
        cols = slice(h * HEAD_DIM, (h + 1) * HEAD_DIM)
        x = x_ref[:, cols].astype(F32)
        ms = _row_meansq(x)
        y = x * lax.rsqrt(ms + EPS) * g_ref[:, cols]
        o_ref[:, cols] = _rope(y, tables, shifts).astype(o_ref.dtype)


def _headnorm_rope(p, col_off, width, gains, rope, seq, tm):
    m = p.shape[0]
    assert col_off % width == 0
    tables, shifts = rope
    cb = col_off // width
    sb = seq // tm
    kern = functools.partial(_headnorm_rope_kernel, nheads=width // HEAD_DIM, shifts=shifts)
    tab_spec = pl.BlockSpec((tm, LANES), lambda i: (i % sb, 0))
    return pl.pallas_call(
        kern,
        grid=(m // tm,),
        in_specs=[
            pl.BlockSpec((tm, width), lambda i: (i, cb)),
            pl.BlockSpec((1, width), lambda i: (0, 0)),
        ] + [tab_spec] * len(tables),
        out_specs=pl.BlockSpec((tm, width), lambda i: (i, 0)),
        out_shape=jax.ShapeDtypeStruct((m, width), BF16),
        compiler_params=_params(
            ("parallel",),
            2 * _nbytes((tm, width), BF16), len(tables) * _nbytes((tm, LANES), F32),
            scratch_bytes=4 * _nbytes((tm, LANES), F32)),
        name="headnorm_rope",
    )(p, gains, *tables)


def _mla_prep_kernel(cq_ref, ckv_ref, kpe_ref, wq_ref, wk_ref, wv_ref, gql_ref, gkl_ref,
                     gqh_ref, gkn_ref, gkr_ref, *rest, shifts):
    tables = [t[...] for t in rest[:-3]]
    q_ref, k_ref, v_ref = rest[-3:]
    sumsq = lambda t: jnp.sum(t * t, axis=-1, keepdims=True)

    cq = cq_ref[...].astype(F32)
    cqn = cq * lax.rsqrt(sumsq(cq) * (1.0 / MLA_Q_RANK) + EPS) * gql_ref[...]
    q = jnp.dot(cqn.astype(BF16), wq_ref[...], preferred_element_type=F32)
    gqh = gqh_ref[...]
    for h in range(MLA_HEADS):
        nope = slice(h * MLA_QK_PAD, h * MLA_QK_PAD + LANES)
        rope = slice(h * MLA_QK_PAD + LANES, (h + 1) * MLA_QK_PAD)
        inv = lax.rsqrt(sumsq(q[:, h * MLA_QK_PAD:(h + 1) * MLA_QK_PAD]) * (1.0 / MLA_QK) + EPS)
        q_ref[:, nope] = (q[:, nope] * inv * gqh[:, :LANES]).astype(q_ref.dtype)
        q_ref[:, rope] = _rope(q[:, rope] * inv * gqh[:, LANES:], tables, shifts).astype(q_ref.dtype)

    ckv = ckv_ref[...].astype(F32)
    ckvn = (ckv * lax.rsqrt(sumsq(ckv) * (1.0 / MLA_KV_RANK) + EPS) * gkl_ref[...]).astype(BF16)
    kn = jnp.dot(ckvn, wk_ref[...], preferred_element_type=F32)
    v_ref[...] = jnp.dot(ckvn, wv_ref[...], preferred_element_type=F32).astype(v_ref.dtype)
    kpe = kpe_ref[...].astype(F32)
    pe_sq = sumsq(kpe)
    gkn = gkn_ref[...]
    kpe_rot = _rope(kpe * gkr_ref[...], tables, shifts)
    for h in range(MLA_HEADS):
        kh = kn[:, h * MLA_NOPE:(h + 1) * MLA_NOPE]
        inv = lax.rsqrt((sumsq(kh) + pe_sq) * (1.0 / MLA_QK) + EPS)
        k_ref[:, h * MLA_QK_PAD:h * MLA_QK_PAD + LANES] = (kh * inv * gkn).astype(k_ref.dtype)
        k_ref[:, h * MLA_QK_PAD + LANES:(h + 1) * MLA_QK_PAD] = (kpe_rot * inv).astype(k_ref.dtype)


def _mla_prep(p, off_cq, off_ckv, off_kpe, wq, wk, wv, gql, gkl, gqh, gkn, gkr, rope, seq, tm):
    m = p.shape[0]
    tables, shifts = rope
    sb = seq // tm
    const = lambda shape: pl.BlockSpec(shape, lambda i: (0, 0))
    tab_spec = pl.BlockSpec((tm, LANES), lambda i: (i % sb, 0))
    qk_cols = MLA_HEADS * MLA_QK_PAD
    v_cols = MLA_HEADS * MLA_V
    return pl.pallas_call(
        functools.partial(_mla_prep_kernel, shifts=shifts),
        grid=(m // tm,),
        in_specs=[
            pl.BlockSpec((tm, MLA_Q_RANK), lambda i: (i, off_cq // MLA_Q_RANK)),
            pl.BlockSpec((tm, MLA_KV_RANK), lambda i: (i, off_ckv // MLA_KV_RANK)),
            pl.BlockSpec((tm, LANES), lambda i: (i, off_kpe // LANES)),
            const(wq.shape), const(wk.shape), const(wv.shape),
            const(gql.shape), const(gkl.shape), const(gqh.shape), const(gkn.shape), const(gkr.shape),
        ] + [tab_spec] * len(tables),
        out_specs=[
            pl.BlockSpec((tm, qk_cols), lambda i: (i, 0)),
            pl.BlockSpec((tm, qk_cols), lambda i: (i, 0)),
            pl.BlockSpec((tm, v_cols), lambda i: (i, 0)),
        ],
        out_shape=[
            jax.ShapeDtypeStruct((m, qk_cols), BF16),
            jax.ShapeDtypeStruct((m, qk_cols), BF16),
            jax.ShapeDtypeStruct((m, v_cols), BF16),
        ],
        compiler_params=_params(
            ("parallel",),
            _nbytes((tm, MLA_Q_RANK + MLA_KV_RANK + LANES), BF16),
            _nbytes(wq.shape, BF16), _nbytes(wk.shape, BF16), _nbytes(wv.shape, BF16),
            _nbytes((tm, 2 * qk_cols + v_cols), BF16), len(tables) * _nbytes((tm, LANES), F32),
            scratch_bytes=3 * _nbytes((tm, qk_cols), F32)),
        name="mla_prep",
    )(p, p, p, wq, wk, wv, gql, gkl, gqh, gkn, gkr, *tables)


def _attn_kernel(q_ref, k_ref, v_ref, o_ref, *, sub, heads, dk, dv, shared_kv):
    for h in range(heads):
        kv = 0 if shared_kv else h
        k = k_ref[:, kv * dk:(kv + 1) * dk]
        v = v_ref[:, kv * dv:(kv + 1) * dv]
        v_ones = jnp.concatenate([v, jnp.ones_like(v)], axis=1)
        for r0 in range(0, q_ref.shape[0], sub):
            rows = slice(r0, r0 + sub)
            s = lax.dot_general(q_ref[rows, h * dk:(h + 1) * dk], k, (((1,), (1,)), ((), ())),
                                preferred_element_type=F32)
            p = jnp.exp2(s - jnp.max(s, axis=-1, keepdims=True)).astype(BF16)
            ov = jnp.dot(p, v_ones, preferred_element_type=F32)
            o_ref[rows, h * dv:(h + 1) * dv] = (ov[:, :dv] * (1.0 / ov[:, dv:])).astype(o_ref.dtype)


def _attention(q_arr, k_arr, v_arr, q_off, k_off, v_off, nheads, group, dk, dv, batch, seq):
    hps = ATTN_HEADS_PER_STEP
    shared_kv = group % hps == 0
    assert nheads % hps == 0 and (shared_kv or group == 1)
    kv_w = 1 if shared_kv else hps
    assert q_off % (hps * dk) == 0 and k_off % (kv_w * dk) == 0 and v_off % (kv_w * dv) == 0
    qb, kb, vb = q_off // (hps * dk), k_off // (kv_w * dk), v_off // (kv_w * dv)
    kv_of = (lambda j: (j * hps) // group) if shared_kv else (lambda j: j)
    sub = min(seq, ATTN_SUB_ROWS)
    kern = functools.partial(_attn_kernel, sub=sub, heads=hps, dk=dk, dv=dv, shared_kv=shared_kv)
    return pl.pallas_call(
        kern,
        grid=(batch, nheads // hps),
        in_specs=[
            pl.BlockSpec((seq, hps * dk), lambda b, j: (b, qb + j)),
            pl.BlockSpec((seq, kv_w * dk), lambda b, j: (b, kb + kv_of(j))),
            pl.BlockSpec((seq, kv_w * dv), lambda b, j: (b, vb + kv_of(j))),
        ],
        out_specs=pl.BlockSpec((seq, hps * dv), lambda b, j: (b, j)),
        out_shape=jax.ShapeDtypeStruct((batch * seq, nheads * dv), BF16),
        compiler_params=_params(
            ("parallel", "parallel"),
            _nbytes((seq, hps * dk), BF16), _nbytes((seq, kv_w * dk), BF16),
            _nbytes((seq, kv_w * dv), BF16), _nbytes((seq, hps * dv), BF16),
            scratch_bytes=4 * _nbytes((sub, seq), F32)),
        name="attention",
    )(q_arr, k_arr, v_arr)


def _dilated_kernel(q0_ref, q1_ref, q2_ref, k0_ref, k1_ref, k2_ref, v0_ref, v1_ref, v2_ref,
                    gq_ref, gk_ref, *rest, seq, patterns, shifts):
    n_tab = len(shifts) + 1
    table_refs, out_ref = rest[:n_tab], rest[n_tab]
    qb_ref, kb_ref, qf_ref, kf_ref, vf_ref, of_ref, lf_ref, s_ref, p_ref, m_ref = rest[n_tab + 1:]
    q_refs, k_refs, v_refs = (q0_ref, q1_ref, q2_ref), (k0_ref, k1_ref, k2_ref), (v0_ref, v1_ref, v2_ref)
    strided = [g for g, (_, dil) in enumerate(patterns) if dil > 1]
    dense = [g for g, (_, dil) in enumerate(patterns) if dil == 1]

    def prepared(src, gain, rows):
        x = src[rows, :].astype(F32)
        ms = _row_meansq(x)
        return _rope(x * lax.rsqrt(ms + EPS) * gain[...], [t[rows, :] for t in table_refs], shifts)

    for g, (window, dil) in enumerate(patterns):
        radius = window // (2 * dil)
        length = seq // dil
        kw = min(length, QBLOCK + 2 * radius)
        slot = strided.index(g) if dil > 1 else dense.index(g)
        for r0 in range(0, seq, DIL_PREP_ROWS):
            chunk = slice(r0, r0 + DIL_PREP_ROWS)
            if dil > 1:
                qf_ref[slot, chunk, :] = prepared(q_refs[g], gq_ref, chunk)
                kf_ref[slot, chunk, :] = prepared(k_refs[g], gk_ref, chunk)
                vf_ref[slot, chunk, :] = v_refs[g][chunk, :].astype(F32)
            else:
                qb_ref[slot, chunk, :] = prepared(q_refs[g], gq_ref, chunk).astype(BF16)
                kb_ref[slot, chunk, :] = prepared(k_refs[g], gk_ref, chunk).astype(BF16)
        def rows(r, start, n, dil=dil):
            return pl.ds(start, n) if dil == 1 else pl.ds(start * dil + r, n, stride=dil)

        def window_of(dense_ref, copy, r, start, n, dil=dil, slot=slot, rows=rows):
            if dil == 1:
                return dense_ref[rows(r, start, n), :]
            return copy[slot, rows(r, start, n), :].astype(BF16)

        tiles = [(r, t * QBLOCK, min(max(t * QBLOCK - radius, 0), length - kw))
                 for r in range(dil) for t in range(length // QBLOCK)]
        for i, (r, q0, ks) in enumerate(tiles):
            q = window_of(qb_ref.at[slot] if dil == 1 else None, qf_ref, r, q0, QBLOCK)
            k = window_of(kb_ref.at[slot] if dil == 1 else None, kf_ref, r, ks, kw)
            s = lax.dot_general(q, k, (((1,), (1,)), ((), ())), preferred_element_type=F32)
            rel = (lax.broadcasted_iota(jnp.int32, s.shape, 0)
                   - lax.broadcasted_iota(jnp.int32, s.shape, 1)) + (q0 - ks)
            s_ref[i, :, :kw] = jnp.where(jnp.abs(rel) <= radius, s, NEG_INF)
        for i in range(len(tiles)):
            s = s_ref[i, :, :kw]
            m = jnp.max(s, axis=-1, keepdims=True)
            p_ref[i, :, :kw] = jnp.exp(s - m).astype(BF16)
            m_ref[i] = jnp.broadcast_to(m, (QBLOCK, HEAD_DIM))
        for i, (r, q0, ks) in enumerate(tiles):
            v = window_of(v_refs[g], vf_ref, r, ks, kw)
            ov = jnp.dot(p_ref[i, :, :kw], jnp.concatenate([v, jnp.ones_like(v)], axis=1),
                         preferred_element_type=F32)
            l = ov[:, HEAD_DIM:]
            of_ref[g, rows(r, q0, QBLOCK), :] = ov[:, :HEAD_DIM] * (1.0 / l)
            lf_ref[g, rows(r, q0, QBLOCK), :] = m_ref[i] + jnp.log(l)

    la, lb, lc = lf_ref[0], lf_ref[1], lf_ref[2]
    m = jnp.maximum(jnp.maximum(la, lb), lc)
    ea, eb, ec = jnp.exp(la - m), jnp.exp(lb - m), jnp.exp(lc - m)
    num = ea * of_ref[0] + eb * of_ref[1] + ec * of_ref[2]
    out_ref[...] = (num * (1.0 / (ea + eb + ec))).astype(out_ref.dtype)


def _dilated(p, off_q, off_k, off_v, gq, gk, rope, batch, seq):
    ng = len(DIL_PATTERNS)
    n_strided = sum(dil > 1 for _, dil in DIL_PATTERNS)
    tables, shifts = rope
    kw_max = max(min(seq // dil, QBLOCK + window // dil) for window, dil in DIL_PATTERNS)
    kern = functools.partial(_dilated_kernel, seq=seq, patterns=DIL_PATTERNS, shifts=shifts)
    group_w = DIL_HPG * HEAD_DIM

    def head(off, g):
        assert off % HEAD_DIM == 0
        cb = (off + g * group_w) // HEAD_DIM
        return pl.BlockSpec((seq, HEAD_DIM), lambda b, h: (b, cb + h))

    const = lambda shape: pl.BlockSpec(shape, lambda b, h: (0, 0))
    slab = _nbytes((seq, HEAD_DIM), F32)
    return pl.pallas_call(
        kern,
        grid=(batch, DIL_HPG),
        in_specs=([head(off_q, g) for g in range(ng)] + [head(off_k, g) for g in range(ng)]
                  + [head(off_v, g) for g in range(ng)]
                  + [const((1, HEAD_DIM))] * 2 + [const((seq, LANES))] * len(tables)),
        out_specs=pl.BlockSpec((seq, HEAD_DIM), lambda b, h: (b, h)),
        out_shape=jax.ShapeDtypeStruct((batch * seq, group_w), BF16),
        scratch_shapes=[pltpu.VMEM((ng - n_strided, seq, HEAD_DIM), BF16)] * 2
                       + [pltpu.VMEM((n_strided, seq, HEAD_DIM), F32)] * 3
                       + [pltpu.VMEM((ng, seq, HEAD_DIM), F32)] * 2
                       + [pltpu.VMEM((seq // QBLOCK, QBLOCK, kw_max), F32),
                          pltpu.VMEM((seq // QBLOCK, QBLOCK, kw_max), BF16),
                          pltpu.VMEM((seq // QBLOCK, QBLOCK, HEAD_DIM), F32)],
        compiler_params=_params(
            ("parallel", "parallel"),
            (3 * ng + 1) * _nbytes((seq, HEAD_DIM), BF16), len(tables) * slab,
            scratch_bytes=(ng - n_strided + 3 * n_strided + 2 * ng) * slab + 4 * slab),
        name="dilated",
    )(*([p] * (3 * ng)), gq, gk, *tables)


def _merge_kernel(xn_ref, oa_ref, ob_ref, oc_ref, wga_ref, wgb_ref, wgc_ref, bga_ref, bgb_ref, bgc_ref,
                  wa_ref, wb_ref, wc_ref, o_ref):
    xn = xn_ref[...]
    merged = None
    for o_i, w_i, wg_i, bg_i in ((oa_ref, wa_ref, wga_ref, bga_ref), (ob_ref, wb_ref, wgb_ref, bgb_ref),
                                 (oc_ref, wc_ref, wgc_ref, bgc_ref)):
        logit = jnp.dot(xn, wg_i[...], preferred_element_type=F32) + bg_i[...]
        gate = 1.0 / (1.0 + jnp.exp(-logit))
        term = gate * jnp.dot(o_i[...], w_i[...], preferred_element_type=F32)
        merged = term if merged is None else merged + term
    o_ref[...] = merged.astype(o_ref.dtype)


def _merge(xn, oa, ob, oc, w_gate, b_gate, w_oa, w_ob, w_oc, layer, tm, tn):
    m, d = xn.shape
    nj = d // tn
    row = lambda a: pl.BlockSpec((tm, a.shape[1]), lambda i, j: (i, 0))
    wcol = lambda w: pl.BlockSpec((None, w.shape[1], tn), lambda i, j: (layer, 0, j))
    wgate = lambda g: pl.BlockSpec((d, tn), lambda i, j: (0, g * nj + j))
    bgate = lambda g: pl.BlockSpec((1, tn), lambda i, j: (0, g * nj + j))
    k_branches = oa.shape[1] + ob.shape[1] + oc.shape[1]
    return pl.pallas_call(
        _merge_kernel,
        grid=(m // tm, nj),
        in_specs=[row(xn), row(oa), row(ob), row(oc), wgate(0), wgate(1), wgate(2),
                  bgate(0), bgate(1), bgate(2), wcol(w_oa), wcol(w_ob), wcol(w_oc)],
        out_specs=pl.BlockSpec((tm, tn), lambda i, j: (i, j)),
        out_shape=jax.ShapeDtypeStruct((m, d), BF16),
        compiler_params=_params(
            ("parallel", "parallel"),
            _nbytes((tm, d + k_branches), BF16), _nbytes((tm, tn), BF16),
            _nbytes((3 * d + k_branches, tn), BF16),
            scratch_bytes=6 * _nbytes((tm, tn), F32)),
        name="merge",
    )(xn, oa, ob, oc, w_gate, w_gate, w_gate, b_gate, b_gate, b_gate, w_oa, w_ob, w_oc)


def _resid_matmul_kernel(x_ref, a_ref, w_ref, o_ref):
    o_ref[...] = x_ref[...] + jnp.dot(a_ref[...], w_ref[...], preferred_element_type=F32)


def _resid_matmul(x2d, a, w, layer, tm, tn):
    m, d = x2d.shape
    k = a.shape[1]
    return pl.pallas_call(
        _resid_matmul_kernel,
        grid=(m // tm, d // tn),
        in_specs=[
            pl.BlockSpec((tm, tn), lambda i, j: (i, j)),
            pl.BlockSpec((tm, k), lambda i, j: (i, 0)),
            pl.BlockSpec((None, k, tn), lambda i, j: (layer, 0, j)),
        ],
        out_specs=pl.BlockSpec((tm, tn), lambda i, j: (i, j)),
        out_shape=jax.ShapeDtypeStruct((m, d), F32),
        compiler_params=_params(
            ("parallel", "parallel"),
            2 * _nbytes((tm, tn), F32), _nbytes((tm, k), BF16), _nbytes((k, tn), BF16)),
        name="out_proj",
    )(x2d, a, w)


def _mlp_kernel(x_ref, g_ref, wu_ref, wd_ref, o_ref, xn_ref):
    c = pl.program_id(1)

    @pl.when(c == 0)
    def _():
        x = x_ref[...]
        ms = jnp.mean(x * x, axis=-1, keepdims=True)
        xn_ref[...] = (x * lax.rsqrt(ms + EPS) * g_ref[...]).astype(BF16)
        o_ref[...] = x

    h = jnp.dot(xn_ref[...], wu_ref[...], preferred_element_type=F32)
    h = jnp.square(jnp.maximum(h, 0.0)).astype(BF16)
    o_ref[...] += jnp.dot(h, wd_ref[...], preferred_element_type=F32)


def _mlp(x2d, gain, w_up, w_down, layer, tm, tf):
    m, d = x2d.shape
    f = w_up.shape[2]
    return pl.pallas_call(
        _mlp_kernel,
        grid=(m // tm, f // tf),
        in_specs=[
            pl.BlockSpec((tm, d), lambda i, c: (i, 0)),
            pl.BlockSpec((1, d), lambda i, c: (0, 0)),
            pl.BlockSpec((None, d, tf), lambda i, c: (layer, 0, c)),
            pl.BlockSpec((None, tf, d), lambda i, c: (layer, c, 0)),
        ],
        out_specs=pl.BlockSpec((tm, d), lambda i, c: (i, 0)),
        out_shape=jax.ShapeDtypeStruct((m, d), F32),
        scratch_shapes=[pltpu.VMEM((tm, d), BF16)],
        compiler_params=_params(
            ("parallel", "arbitrary"),
            2 * _nbytes((tm, d), F32), 2 * _nbytes((d, tf), BF16),
            scratch_bytes=_nbytes((tm, d), BF16) + _nbytes((tm, tf), F32)),
        name="mlp",
    )(x2d, gain, w_up, w_down)


def _largest_tile(n, cap):
    t = min(n, cap)
    while n % t:
        t //= 2
    return t


def kernel(x, attn_norm, w_in, b_gate, mla_q_lat_norm, w_uq, mla_kv_lat_norm, w_ukv,
           mla_q_head_norm, mla_k_head_norm, gqa_q_norm, gqa_k_norm, dil_q_norm, dil_k_norm,
           w_oa, w_ob, w_oc, w_out, mlp_norm, w_up, w_down):
    batch, seq, d = x.shape
    depth = w_in.shape[0]
    m = batch * seq
    d_ff = w_up.shape[2]
    gate_cols = 3 * d
    assert w_in.shape[2] == A_COLS + B_COLS + C_COLS + gate_cols
    assert seq % GRID_W == 0 and seq % (16 * QBLOCK) == 0 and d % LANES == 0

    tn_proj = 1792
    nq_b, nk_b = GQA_HEADS * HEAD_DIM, GQA_KV_HEADS * HEAD_DIM
    nh_c = DIL_HEADS * HEAD_DIM
    off_bq = 0
    off_bk = off_bq + nq_b
    off_bv = off_bk + nk_b
    off_cq = off_bv + nk_b
    off_ck = off_cq + nh_c
    off_cv = off_ck + nh_c
    off_aq = off_cv + nh_c
    off_akv = off_aq + MLA_Q_RANK
    off_ape = off_akv + MLA_KV_RANK
    n_proj = _round_up(off_ape + LANES, tn_proj)

    tm_big = _largest_tile(m, 1024)
    tm_half = _largest_tile(m, 512)
    tm_seq = _largest_tile(seq, 1024)
    tm_mla = _largest_tile(seq, 512)

    pos = np.arange(seq, dtype=np.float64)
    row = np.repeat(np.arange(seq // GRID_W, dtype=np.float64), GRID_W)
    col = np.tile(np.arange(GRID_W, dtype=np.float64), seq // GRID_W)
    half_mla, half_ax, half_dil = MLA_ROPE // 2, HEAD_DIM // 4, PARTIAL_ROPE_DIM // 2
    rope_mla = _rope_tables(seq, half_mla, [(0, half_mla, pos, MLA_ROPE_THETA)])
    rope_gqa = _rope_tables(seq, half_ax, [(0, half_ax, row, AXIAL_THETA),
                                           (2 * half_ax, half_ax, col, AXIAL_THETA)])
    rope_dil = _rope_tables(seq, half_dil, [(0, half_dil, pos, PARTIAL_ROPE_THETA)])

    w_oa_b, w_ob_b, w_oc_b, w_out_b = (w.astype(BF16) for w in (w_oa, w_ob, w_oc, w_out))
    w_up_b, w_down_b = w_up.astype(BF16), w_down.astype(BF16)
    qk_pad = MLA_QK_PAD - MLA_QK

    x2d = x.reshape(m, d)
    for l in range(depth):
        w_p, w_g = _split_w_in(w_in, l, A_COLS, B_COLS + C_COLS, n_proj, _largest_tile(d, 256))
        p, xn = _proj(x2d, attn_norm[l][None, :], w_p, tm_big, tn_proj)

        wq = jnp.pad(w_uq[l].reshape(MLA_Q_RANK, MLA_HEADS, MLA_QK), ((0, 0), (0, 0), (0, qk_pad))
                     ).reshape(MLA_Q_RANK, MLA_HEADS * MLA_QK_PAD).astype(BF16)
        wkv = w_ukv[l].reshape(MLA_KV_RANK, MLA_HEADS, MLA_NOPE + MLA_V)
        wk = wkv[:, :, :MLA_NOPE].reshape(MLA_KV_RANK, MLA_HEADS * MLA_NOPE).astype(BF16)
        wv = wkv[:, :, MLA_NOPE:].reshape(MLA_KV_RANK, MLA_HEADS * MLA_V).astype(BF16)
        gqh = jnp.pad(mla_q_head_norm[l] * (LOG2_E * MLA_QK ** -0.5), (0, qk_pad))[None, :]
        gkn = mla_k_head_norm[l][None, :MLA_NOPE]
        gkr = jnp.pad(mla_k_head_norm[l][MLA_NOPE:], (0, LANES - MLA_ROPE))[None, :]
        qa, ka, va = _mla_prep(p, off_aq, off_akv, off_ape, wq, wk, wv,
                               mla_q_lat_norm[l][None, :], mla_kv_lat_norm[l][None, :],
                               gqh, gkn, gkr, rope_mla, seq, tm_mla)
        oa = _attention(qa, ka, va, 0, 0, 0, MLA_HEADS, 1, MLA_QK_PAD, MLA_V, batch, seq)

        scale = HEAD_DIM ** -0.5
        gq = jnp.tile(gqa_q_norm[l] * (LOG2_E * scale), GQA_HEADS)[None, :]
        gk = jnp.tile(gqa_k_norm[l], GQA_KV_HEADS)[None, :]
        qb = _headnorm_rope(p, off_bq, nq_b, gq, rope_gqa, seq, tm_seq)
        kb = _headnorm_rope(p, off_bk, nk_b, gk, rope_gqa, seq, tm_seq)
        ob = _attention(qb, kb, p, 0, 0, off_bv, GQA_HEADS, GQA_GROUP, HEAD_DIM, HEAD_DIM,
                        batch, seq)

        oc = _dilated(p, off_cq, off_ck, off_cv, (dil_q_norm[l] * scale)[None, :],
                      dil_k_norm[l][None, :], rope_dil, batch, seq)

        merged = _merge(xn, oa, ob, oc, w_g, b_gate[l][None, :], w_oa_b, w_ob_b, w_oc_b, l,
                        tm_big, _largest_tile(d, 512))
        x2d = _resid_matmul(x2d, merged, w_out_b, l, tm_half, d)
        x2d = _mlp(x2d, mlp_norm[l][None, :], w_up_b, w_down_b, l,
                   _largest_tile(m, 1024), _largest_tile(d_ff, 512))
    return x2d.reshape(batch, seq, d)
```

```python
import functools

import numpy as np
import jax
import jax.numpy as jnp
from jax import lax
from jax.experimental import pallas as pl
from jax.experimental.pallas import tpu as pltpu

F32 = jnp.float32
BF16 = jnp.bfloat16

LANES = 128
VMEM_CAP_BYTES = 60000 * 1024

HEAD_DIM = 128
GRID_W = 64
EPS = 1e-6
NEG_INF = -1e30
NT_DIMS = (((1,), (1,)), ((), ()))

MLA_HEADS = 8
MLA_Q_RANK = 512
MLA_KV_RANK = 256
MLA_NOPE = 128
MLA_ROPE = 64
MLA_V = 128
MLA_ROPE_THETA = 10000.0
MLA_QK = MLA_NOPE + MLA_ROPE
MLA_QK_PAD = 2 * LANES

GQA_HEADS = 8
GQA_KV_HEADS = 2
GQA_GROUP = GQA_HEADS // GQA_KV_HEADS
AXIAL_THETA = 10000.0

DIL_PATTERNS = ((128, 1), (512, 4), (2048, 16))
DIL_HPG = 4
DIL_HEADS = DIL_HPG * len(DIL_PATTERNS)
PARTIAL_ROPE_DIM = HEAD_DIM // 4
PARTIAL_ROPE_THETA = 500000.0
QBLOCK = 128
ATTN_SUB_ROWS = 256
LOG2_E = 1.4426950408889634
ATTN_HEADS_PER_STEP = 2
DIL_PREP_ROWS = 256

A_COLS = MLA_Q_RANK + MLA_KV_RANK + MLA_ROPE
B_COLS = (GQA_HEADS + 2 * GQA_KV_HEADS) * HEAD_DIM
C_COLS = 3 * DIL_HEADS * HEAD_DIM


def _round_up(n, m):
    return -(-n // m) * m


def _params(semantics, *block_bytes, scratch_bytes=0):
    need = 2 * sum(block_bytes) + scratch_bytes
    limit = min(VMEM_CAP_BYTES, need + (24 << 20))
    return pltpu.CompilerParams(dimension_semantics=semantics, vmem_limit_bytes=limit)


def _nbytes(shape, dtype):
    n = 1
    for s in shape:
        n *= s
    return n * jnp.dtype(dtype).itemsize


def _rope_tables(seq, dist, segments):
    cos = np.ones((seq, LANES), np.float64)
    sin_x1 = np.zeros((seq, LANES), np.float64)
    sin_x2 = np.zeros((seq, LANES), np.float64)
    for first, half, pos, theta in segments:
        inv = theta ** (-np.arange(half, dtype=np.float64) / half)
        ang = pos[:, None] * inv[None, :]
        c, s = np.cos(ang), np.sin(ang)
        hi = first + dist
        cos[:, first:first + half] = c
        cos[:, hi:hi + half] = c
        sin_x1[:, first:first + half] = -s
        sin_x2[:, hi:hi + half] = s
    as_table = lambda t: jnp.asarray(t.astype(np.float32))
    if 2 * dist == LANES:
        return (as_table(cos), as_table(sin_x1 + sin_x2)), (dist,)
    return (as_table(cos), as_table(sin_x1), as_table(sin_x2)), (LANES - dist, dist)


def _rope(x, tables, shifts):
    out = x * tables[0]
    for table, shift in zip(tables[1:], shifts):
        out = out + pltpu.roll(x, shift, 1) * table
    return out


def _proj_kernel(x_ref, g_ref, w_ref, o_ref, xn_ref):
    @pl.when(pl.program_id(1) == 0)
    def _():
        x = x_ref[...]
        ms = jnp.mean(x * x, axis=-1, keepdims=True)
        xn_ref[...] = (x * lax.rsqrt(ms + EPS) * g_ref[...]).astype(BF16)

    o_ref[...] = lax.dot_general(xn_ref[...], w_ref[...], NT_DIMS,
                                 preferred_element_type=F32).astype(o_ref.dtype)


def _proj(x2d, gain, w_t, tm, tn):
    m, d = x2d.shape
    n = w_t.shape[0]
    return pl.pallas_call(
        _proj_kernel,
        grid=(m // tm, n // tn),
        in_specs=[
            pl.BlockSpec((tm, d), lambda i, j: (i, 0)),
            pl.BlockSpec((1, d), lambda i, j: (0, 0)),
            pl.BlockSpec((tn, d), lambda i, j: (j, 0)),
        ],
        out_specs=[pl.BlockSpec((tm, tn), lambda i, j: (i, j)),
                   pl.BlockSpec((tm, d), lambda i, j: (i, 0))],
        out_shape=[jax.ShapeDtypeStruct((m, n), BF16), jax.ShapeDtypeStruct((m, d), BF16)],
        compiler_params=_params(
            ("parallel", "arbitrary"),
            _nbytes((tm, d), F32), _nbytes((d, tn), BF16), _nbytes((tm, tn), BF16),
            _nbytes((tm, d), BF16), scratch_bytes=_nbytes((tm, d), F32)),
        name="proj",
    )(x2d, gain, w_t)


def _row_meansq(x):
    n = x.shape[-1]
    assert n & (n - 1) == 0
    sq = x * x
    hi = sq.astype(BF16)
    lo = (sq - hi.astype(F32)).astype(BF16)
    weights = jnp.full((n, LANES), 1.0 / n, BF16)
    return (jnp.dot(hi, weights, preferred_element_type=F32)
            + jnp.dot(lo, weights, preferred_element_type=F32))


def _headnorm_rope_kernel(x_ref, g_ref, *rest, nheads, shifts):
    table_refs, o_ref = rest[:-1], rest[-1]
    tables = [t[...] for t in table_refs]
    for h in range(nheads):
        cols = slice(h * HEAD_DIM, (h + 1) * HEAD_DIM)
        x = x_ref[:, cols].astype(F32)
        ms = _row_meansq(x)
        y = x * lax.rsqrt(ms + EPS) * g_ref[:, cols]
        o_ref[:, cols] = _rope(y, tables, shifts).astype(o_ref.dtype)


def _headnorm_rope(p, col_off, width, gains, rope, seq, tm):
    m = p.shape[0]
    assert col_off % width == 0
    tables, shifts = rope
    cb = col_off // width
    sb = seq // tm
    kern = functools.partial(_headnorm_rope_kernel, nheads=width // HEAD_DIM, shifts=shifts)
    tab_spec = pl.BlockSpec((tm, LANES), lambda i: (i % sb, 0))
    return pl.pallas_call(
        kern,
        grid=(m // tm,),
        in_specs=[
            pl.BlockSpec((tm, width), lambda i: (i, cb)),
            pl.BlockSpec((1, width), lambda i: (0, 0)),
        ] + [tab_spec] * len(tables),
        out_specs=pl.BlockSpec((tm, width), lambda i: (i, 0)),
        out_shape=jax.ShapeDtypeStruct((m, width), BF16),
        compiler_params=_params(
            ("parallel",),
            2 * _nbytes((tm, width), BF16), len(tables) * _nbytes((tm, LANES), F32),
            scratch_bytes=4 * _nbytes((tm, LANES), F32)),
        name="headnorm_rope",
    )(p, gains, *tables)


def _mla_prep_kernel(cq_ref, ckv_ref, kpe_ref, wq_ref, wk_ref, wv_ref, gql_ref, gkl_ref,
                     gqh_ref, gkn_ref, gkr_ref, *rest, shifts):
    tables = [t[...] for t in rest[:-3]]
    q_ref, k_ref, v_ref = rest[-3:]
    sumsq = lambda t: jnp.sum(t * t, axis=-1, keepdims=True)

    cq = cq_ref[...].astype(F32)
    cqn = cq * lax.rsqrt(sumsq(cq) * (1.0 / MLA_Q_RANK) + EPS) * gql_ref[...]
    q = jnp.dot(cqn.astype(BF16), wq_ref[...], preferred_element_type=F32)
    gqh = gqh_ref[...]
    for h in range(MLA_HEADS):
        nope = slice(h * MLA_QK_PAD, h * MLA_QK_PAD + LANES)
        rope = slice(h * MLA_QK_PAD + LANES, (h + 1) * MLA_QK_PAD)
        inv = lax.rsqrt(sumsq(q[:, h * MLA_QK_PAD:(h + 1) * MLA_QK_PAD]) * (1.0 / MLA_QK) + EPS)
        q_ref[:, nope] = (q[:, nope] * inv * gqh[:, :LANES]).astype(q_ref.dtype)
        q_ref[:, rope] = _rope(q[:, rope] * inv * gqh[:, LANES:], tables, shifts).astype(q_ref.dtype)

    ckv = ckv_ref[...].astype(F32)
    ckvn = (ckv * lax.rsqrt(sumsq(ckv) * (1.0 / MLA_KV_RANK) + EPS) * gkl_ref[...]).astype(BF16)
    kn = jnp.dot(ckvn, wk_ref[...], preferred_element_type=F32)
    v_ref[...] = jnp.dot(ckvn, wv_ref[...], preferred_element_type=F32).astype(v_ref.dtype)
    kpe = kpe_ref[...].astype(F32)
    pe_sq = sumsq(kpe)
    gkn = gkn_ref[...]
    kpe_rot = _rope(kpe * gkr_ref[...], tables, shifts)
    for h in range(MLA_HEADS):
        kh = kn[:, h * MLA_NOPE:(h + 1) * MLA_NOPE]
        inv = lax.rsqrt((sumsq(kh) + pe_sq) * (1.0 / MLA_QK) + EPS)
        k_ref[:, h * MLA_QK_PAD:h * MLA_QK_PAD + LANES] = (kh * inv * gkn).astype(k_ref.dtype)
        k_ref[:, h * MLA_QK_PAD + LANES:(h + 1) * MLA_QK_PAD] = (kpe_rot * inv).astype(k_ref.dtype)


def _mla_prep(p, off_cq, off_ckv, off_kpe, wq, wk, wv, gql, gkl, gqh, gkn, gkr, rope, seq, tm):
    m = p.shape[0]
    tables, shifts = rope
    sb = seq // tm
    const = lambda shape: pl.BlockSpec(shape, lambda i: (0, 0))
    tab_spec = pl.BlockSpec((tm, LANES), lambda i: (i % sb, 0))
    qk_cols = MLA_HEADS * MLA_QK_PAD
    v_cols = MLA_HEADS * MLA_V
    return pl.pallas_call(
        functools.partial(_mla_prep_kernel, shifts=shifts),
        grid=(m // tm,),
        in_specs=[
            pl.BlockSpec((tm, MLA_Q_RANK), lambda i: (i, off_cq // MLA_Q_RANK)),
            pl.BlockSpec((tm, MLA_KV_RANK), lambda i: (i, off_ckv // MLA_KV_RANK)),
            pl.BlockSpec((tm, LANES), lambda i: (i, off_kpe // LANES)),
            const(wq.shape), const(wk.shape), const(wv.shape),
            const(gql.shape), const(gkl.shape), const(gqh.shape), const(gkn.shape), const(gkr.shape),
        ] + [tab_spec] * len(tables),
        out_specs=[
            pl.BlockSpec((tm, qk_cols), lambda i: (i, 0)),
            pl.BlockSpec((tm, qk_cols), lambda i: (i, 0)),
            pl.BlockSpec((tm, v_cols), lambda i: (i, 0)),
        ],
        out_shape=[
            jax.ShapeDtypeStruct((m, qk_cols), BF16),
            jax.ShapeDtypeStruct((m, qk_cols), BF16),
            jax.ShapeDtypeStruct((m, v_cols), BF16),
        ],
        compiler_params=_params(
            ("parallel",),
            _nbytes((tm, MLA_Q_RANK + MLA_KV_RANK + LANES), BF16),
            _nbytes(wq.shape, BF16), _nbytes(wk.shape, BF16), _nbytes(wv.shape, BF16),
            _nbytes((tm, 2 * qk_cols + v_cols), BF16), len(tables) * _nbytes((tm, LANES), F32),
            scratch_bytes=3 * _nbytes((tm, qk_cols), F32)),
        name="mla_prep",
    )(p, p, p, wq, wk, wv, gql, gkl, gqh, gkn, gkr, *tables)


def _attn_kernel(q_ref, k_ref, v_ref, o_ref, *, sub, heads, dk, dv, shared_kv):
    for h in range(heads):
        kv = 0 if shared_kv else h
        k = k_ref[:, kv * dk:(kv + 1) * dk]
        v = v_ref[:, kv * dv:(kv + 1) * dv]
        v_ones = jnp.concatenate([v, jnp.ones_like(v)], axis=1)
        for r0 in range(0, q_ref.shape[0], sub):
            rows = slice(r0, r0 + sub)
            s = lax.dot_general(q_ref[rows, h * dk:(h + 1) * dk], k, (((1,), (1,)), ((), ())),
                                preferred_element_type=F32)
            p = jnp.exp2(s - jnp.max(s, axis=-1, keepdims=True)).astype(BF16)
            ov = jnp.dot(p, v_ones, preferred_element_type=F32)
            o_ref[rows, h * dv:(h + 1) * dv] = (ov[:, :dv] * (1.0 / ov[:, dv:])).astype(o_ref.dtype)


def _attention(q_arr, k_arr, v_arr, q_off, k_off, v_off, nheads, group, dk, dv, batch, seq):
    hps = ATTN_HEADS_PER_STEP
    shared_kv = group % hps == 0
    assert nheads % hps == 0 and (shared_kv or group == 1)
    kv_w = 1 if shared_kv else hps
    assert q_off % (hps * dk) == 0 and k_off % (kv_w * dk) == 0 and v_off % (kv_w * dv) == 0
    qb, kb, vb = q_off // (hps * dk), k_off // (kv_w * dk), v_off // (kv_w * dv)
    kv_of = (lambda j: (j * hps) // group) if shared_kv else (lambda j: j)
    sub = min(seq, ATTN_SUB_ROWS)
    kern = functools.partial(_attn_kernel, sub=sub, heads=hps, dk=dk, dv=dv, shared_kv=shared_kv)
    return pl.pallas_call(
        kern,
        grid=(batch, nheads // hps),
        in_specs=[
            pl.BlockSpec((seq, hps * dk), lambda b, j: (b, qb + j)),
            pl.BlockSpec((seq, kv_w * dk), lambda b, j: (b, kb + kv_of(j))),
            pl.BlockSpec((seq, kv_w * dv), lambda b, j: (b, vb + kv_of(j))),
        ],
        out_specs=pl.BlockSpec((seq, hps * dv), lambda b, j: (b, j)),
        out_shape=jax.ShapeDtypeStruct((batch * seq, nheads * dv), BF16),
        compiler_params=_params(
            ("parallel", "parallel"),
            _nbytes((seq, hps * dk), BF16), _nbytes((seq, kv_w * dk), BF16),
            _nbytes((seq, kv_w * dv), BF16), _nbytes((seq, hps * dv), BF16),
            scratch_bytes=4 * _nbytes((sub, seq), F32)),
        name="attention",
    )(q_arr, k_arr, v_arr)


def _dilated_kernel(q0_ref, q1_ref, q2_ref, k0_ref, k1_ref, k2_ref, v0_ref, v1_ref, v2_ref,
                    gq_ref, gk_ref, *rest, seq, patterns, shifts):
    n_tab = len(shifts) + 1
    table_refs, out_ref = rest[:n_tab], rest[n_tab]
    qb_ref, kb_ref, qf_ref, kf_ref, vf_ref, of_ref, lf_ref, s_ref, p_ref, m_ref = rest[n_tab + 1:]
    q_refs, k_refs, v_refs = (q0_ref, q1_ref, q2_ref), (k0_ref, k1_ref, k2_ref), (v0_ref, v1_ref, v2_ref)
    strided = [g for g, (_, dil) in enumerate(patterns) if dil > 1]
    dense = [g for g, (_, dil) in enumerate(patterns) if dil == 1]

    def prepared(src, gain, rows):
        x = src[rows, :].astype(F32)
        ms = _row_meansq(x)
        return _rope(x * lax.rsqrt(ms + EPS) * gain[...], [t[rows, :] for t in table_refs], shifts)

    for g, (window, dil) in enumerate(patterns):
        radius = window // (2 * dil)
        length = seq // dil
        kw = min(length, QBLOCK + 2 * radius)
        slot = strided.index(g) if dil > 1 else dense.index(g)
        for r0 in range(0, seq, DIL_PREP_ROWS):
            chunk = slice(r0, r0 + DIL_PREP_ROWS)
            if dil > 1:
                qf_ref[slot, chunk, :] = prepared(q_refs[g], gq_ref, chunk)
                kf_ref[slot, chunk, :] = prepared(k_refs[g], gk_ref, chunk)
                vf_ref[slot, chunk, :] = v_refs[g][chunk, :].astype(F32)
            else:
                qb_ref[slot, chunk, :] = prepared(q_refs[g], gq_ref, chunk).astype(BF16)
                kb_ref[slot, chunk, :] = prepared(k_refs[g], gk_ref, chunk).astype(BF16)
        def rows(r, start, n, dil=dil):
            return pl.ds(start, n) if dil == 1 else pl.ds(start * dil + r, n, stride=dil)

        def window_of(dense_ref, copy, r, start, n, dil=dil, slot=slot, rows=rows):
            if dil == 1:
                return dense_ref[rows(r, start, n), :]
            return copy[slot, rows(r, start, n), :].astype(BF16)

        tiles = [(r, t * QBLOCK, min(max(t * QBLOCK - radius, 0), length - kw))
                 for r in range(dil) for t in range(length // QBLOCK)]
        for i, (r, q0, ks) in enumerate(tiles):
            q = window_of(qb_ref.at[slot] if dil == 1 else None, qf_ref, r, q0, QBLOCK)
            k = window_of(kb_ref.at[slot] if dil == 1 else None, kf_ref, r, ks, kw)
            s = lax.dot_general(q, k, (((1,), (1,)), ((), ())), preferred_element_type=F32)
            rel = (lax.broadcasted_iota(jnp.int32, s.shape, 0)
                   - lax.broadcasted_iota(jnp.int32, s.shape, 1)) + (q0 - ks)
            s_ref[i, :, :kw] = jnp.where(jnp.abs(rel) <= radius, s, NEG_INF)
        for i in range(len(tiles)):
            s = s_ref[i, :, :kw]
            m = jnp.max(s, axis=-1, keepdims=True)
            p_ref[i, :, :kw] = jnp.exp(s - m).astype(BF16)
            m_ref[i] = jnp.broadcast_to(m, (QBLOCK, HEAD_DIM))
        for i, (r, q0, ks) in enumerate(tiles):
            v = window_of(v_refs[g], vf_ref, r, ks, kw)
            ov = jnp.dot(p_ref[i, :, :kw], jnp.concatenate([v, jnp.ones_like(v)], axis=1),
                         preferred_element_type=F32)
            l = ov[:, HEAD_DIM:]
            of_ref[g, rows(r, q0, QBLOCK), :] = ov[:, :HEAD_DIM] * (1.0 / l)
            lf_ref[g, rows(r, q0, QBLOCK), :] = m_ref[i] + jnp.log(l)

    la, lb, lc = lf_ref[0], lf_ref[1], lf_ref[2]
    m = jnp.maximum(jnp.maximum(la, lb), lc)
    ea, eb, ec = jnp.exp(la - m), jnp.exp(lb - m), jnp.exp(lc - m)
    num = ea * of_ref[0] + eb * of_ref[1] + ec * of_ref[2]
    out_ref[...] = (num * (1.0 / (ea + eb + ec))).astype(out_ref.dtype)


def _dilated(p, off_q, off_k, off_v, gq, gk, rope, batch, seq):
    ng = len(DIL_PATTERNS)
    n_strided = sum(dil > 1 for _, dil in DIL_PATTERNS)
    tables, shifts = rope
    kw_max = max(min(seq // dil, QBLOCK + window // dil) for window, dil in DIL_PATTERNS)
    kern = functools.partial(_dilated_kernel, seq=seq, patterns=DIL_PATTERNS, shifts=shifts)
    group_w = DIL_HPG * HEAD_DIM

    def head(off, g):
        assert off % HEAD_DIM == 0
        cb = (off + g * group_w) // HEAD_DIM
        return pl.BlockSpec((seq, HEAD_DIM), lambda b, h: (b, cb + h))

    const = lambda shape: pl.BlockSpec(shape, lambda b, h: (0, 0))
    slab = _nbytes((seq, HEAD_DIM), F32)
    return pl.pallas_call(
        kern,
        grid=(batch, DIL_HPG),
        in_specs=([head(off_q, g) for g in range(ng)] + [head(off_k, g) for g in range(ng)]
                  + [head(off_v, g) for g in range(ng)]
                  + [const((1, HEAD_DIM))] * 2 + [const((seq, LANES))] * len(tables)),
        out_specs=pl.BlockSpec((seq, HEAD_DIM), lambda b, h: (b, h)),
        out_shape=jax.ShapeDtypeStruct((batch * seq, group_w), BF16),
        scratch_shapes=[pltpu.VMEM((ng - n_strided, seq, HEAD_DIM), BF16)] * 2
                       + [pltpu.VMEM((n_strided, seq, HEAD_DIM), F32)] * 3
                       + [pltpu.VMEM((ng, seq, HEAD_DIM), F32)] * 2
                       + [pltpu.VMEM((seq // QBLOCK, QBLOCK, kw_max), F32),
                          pltpu.VMEM((seq // QBLOCK, QBLOCK, kw_max), BF16),
                          pltpu.VMEM((seq // QBLOCK, QBLOCK, HEAD_DIM), F32)],
        compiler_params=_params(
            ("parallel", "parallel"),
            (3 * ng + 1) * _nbytes((seq, HEAD_DIM), BF16), len(tables) * slab,
            scratch_bytes=(ng - n_strided + 3 * n_strided + 2 * ng) * slab + 4 * slab),
        name="dilated",
    )(*([p] * (3 * ng)), gq, gk, *tables)


def _merge_kernel(xn_ref, oa_ref, ob_ref, oc_ref, wga_ref, wgb_ref, wgc_ref, bga_ref, bgb_ref, bgc_ref,
                  wa_ref, wb_ref, wc_ref, o_ref):
    xn = xn_ref[...]
    merged = None
    for o_i, w_i, wg_i, bg_i in ((oa_ref, wa_ref, wga_ref, bga_ref), (ob_ref, wb_ref, wgb_ref, bgb_ref),
                                 (oc_ref, wc_ref, wgc_ref, bgc_ref)):
        logit = lax.dot_general(xn, wg_i[...], NT_DIMS, preferred_element_type=F32) + bg_i[...]
        gate = 1.0 / (1.0 + jnp.exp(-logit))
        term = gate * jnp.dot(o_i[...], w_i[...], preferred_element_type=F32)
        merged = term if merged is None else merged + term
    o_ref[...] = merged.astype(o_ref.dtype)


def _merge(xn, oa, ob, oc, w_gate, b_gate, w_oa, w_ob, w_oc, layer, tm, tn):
    m, d = xn.shape
    nj = d // tn
    row = lambda a: pl.BlockSpec((tm, a.shape[1]), lambda i, j: (i, 0))
    wcol = lambda w: pl.BlockSpec((None, w.shape[1], tn), lambda i, j: (layer, 0, j))
    wgate = lambda g: pl.BlockSpec((tn, d), lambda i, j: (g * nj + j, 0))
    bgate = lambda g: pl.BlockSpec((1, tn), lambda i, j: (0, g * nj + j))
    k_branches = oa.shape[1] + ob.shape[1] + oc.shape[1]
    return pl.pallas_call(
        _merge_kernel,
        grid=(m // tm, nj),
        in_specs=[row(xn), row(oa), row(ob), row(oc), wgate(0), wgate(1), wgate(2),
                  bgate(0), bgate(1), bgate(2), wcol(w_oa), wcol(w_ob), wcol(w_oc)],
        out_specs=pl.BlockSpec((tm, tn), lambda i, j: (i, j)),
        out_shape=jax.ShapeDtypeStruct((m, d), BF16),
        compiler_params=_params(
            ("parallel", "parallel"),
            _nbytes((tm, d + k_branches), BF16), _nbytes((tm, tn), BF16),
            _nbytes((3 * d + k_branches, tn), BF16),
            scratch_bytes=6 * _nbytes((tm, tn), F32)),
        name="merge",
    )(xn, oa, ob, oc, w_gate, w_gate, w_gate, b_gate, b_gate, b_gate, w_oa, w_ob, w_oc)


def _resid_matmul_kernel(x_ref, a_ref, w_ref, o_ref):
    o_ref[...] = x_ref[...] + jnp.dot(a_ref[...], w_ref[...], preferred_element_type=F32)


def _resid_matmul(x2d, a, w, layer, tm, tn):
    m, d = x2d.shape
    k = a.shape[1]
    return pl.pallas_call(
        _resid_matmul_kernel,
        grid=(m // tm, d // tn),
        in_specs=[
            pl.BlockSpec((tm, tn), lambda i, j: (i, j)),
            pl.BlockSpec((tm, k), lambda i, j: (i, 0)),
            pl.BlockSpec((None, k, tn), lambda i, j: (layer, 0, j)),
        ],
        out_specs=pl.BlockSpec((tm, tn), lambda i, j: (i, j)),
        out_shape=jax.ShapeDtypeStruct((m, d), F32),
        compiler_params=_params(
            ("parallel", "parallel"),
            2 * _nbytes((tm, tn), F32), _nbytes((tm, k), BF16), _nbytes((k, tn), BF16)),
        name="out_proj",
    )(x2d, a, w)


def _mlp_kernel(x_ref, g_ref, wu_ref, wd_ref, o_ref, xn_ref):
    c = pl.program_id(1)

    @pl.when(c == 0)
    def _():
        x = x_ref[...]
        ms = jnp.mean(x * x, axis=-1, keepdims=True)
        xn_ref[...] = (x * lax.rsqrt(ms + EPS) * g_ref[...]).astype(BF16)
        o_ref[...] = x

    h = jnp.dot(xn_ref[...], wu_ref[...], preferred_element_type=F32)
    h = jnp.square(jnp.maximum(h, 0.0)).astype(BF16)
    o_ref[...] += jnp.dot(h, wd_ref[...], preferred_element_type=F32)


def _mlp(x2d, gain, w_up, w_down, layer, tm, tf):
    m, d = x2d.shape
    f = w_up.shape[2]
    return pl.pallas_call(
        _mlp_kernel,
        grid=(m // tm, f // tf),
        in_specs=[
            pl.BlockSpec((tm, d), lambda i, c: (i, 0)),
            pl.BlockSpec((1, d), lambda i, c: (0, 0)),
            pl.BlockSpec((None, d, tf), lambda i, c: (layer, 0, c)),
            pl.BlockSpec((None, tf, d), lambda i, c: (layer, c, 0)),
        ],
        out_specs=pl.BlockSpec((tm, d), lambda i, c: (i, 0)),
        out_shape=jax.ShapeDtypeStruct((m, d), F32),
        scratch_shapes=[pltpu.VMEM((tm, d), BF16)],
        compiler_params=_params(
            ("parallel", "arbitrary"),
            2 * _nbytes((tm, d), F32), 2 * _nbytes((d, tf), BF16),
            scratch_bytes=_nbytes((tm, d), BF16) + _nbytes((tm, tf), F32)),
        name="mlp",
    )(x2d, gain, w_up, w_down)


def _largest_tile(n, cap):
    t = min(n, cap)
    while n % t:
        t //= 2
    return t


def kernel(x, attn_norm, w_in, b_gate, mla_q_lat_norm, w_uq, mla_kv_lat_norm, w_ukv,
           mla_q_head_norm, mla_k_head_norm, gqa_q_norm, gqa_k_norm, dil_q_norm, dil_k_norm,
           w_oa, w_ob, w_oc, w_out, mlp_norm, w_up, w_down):
    batch, seq, d = x.shape
    depth = w_in.shape[0]
    m = batch * seq
    d_ff = w_up.shape[2]
    gate_cols = 3 * d
    assert w_in.shape[2] == A_COLS + B_COLS + C_COLS + gate_cols
    assert seq % GRID_W == 0 and seq % (16 * QBLOCK) == 0 and d % LANES == 0

    tn_proj = 1792
    nq_b, nk_b = GQA_HEADS * HEAD_DIM, GQA_KV_HEADS * HEAD_DIM
    nh_c = DIL_HEADS * HEAD_DIM
    off_bq = 0
    off_bk = off_bq + nq_b
    off_bv = off_bk + nk_b
    off_cq = off_bv + nk_b
    off_ck = off_cq + nh_c
    off_cv = off_ck + nh_c
    off_aq = off_cv + nh_c
    off_akv = off_aq + MLA_Q_RANK
    off_ape = off_akv + MLA_KV_RANK
    n_proj = _round_up(off_ape + LANES, tn_proj)

    tm_big = _largest_tile(m, 1024)
    tm_half = _largest_tile(m, 512)
    tm_seq = _largest_tile(seq, 1024)
    tm_mla = _largest_tile(seq, 512)

    pos = np.arange(seq, dtype=np.float64)
    row = np.repeat(np.arange(seq // GRID_W, dtype=np.float64), GRID_W)
    col = np.tile(np.arange(GRID_W, dtype=np.float64), seq // GRID_W)
    half_mla, half_ax, half_dil = MLA_ROPE // 2, HEAD_DIM // 4, PARTIAL_ROPE_DIM // 2
    rope_mla = _rope_tables(seq, half_mla, [(0, half_mla, pos, MLA_ROPE_THETA)])
    rope_gqa = _rope_tables(seq, half_ax, [(0, half_ax, row, AXIAL_THETA),
                                           (2 * half_ax, half_ax, col, AXIAL_THETA)])
    rope_dil = _rope_tables(seq, half_dil, [(0, half_dil, pos, PARTIAL_ROPE_THETA)])

    w_oa_b, w_ob_b, w_oc_b, w_out_b = (w.astype(BF16) for w in (w_oa, w_ob, w_oc, w_out))
    w_up_b, w_down_b = w_up.astype(BF16), w_down.astype(BF16)
    qk_pad = MLA_QK_PAD - MLA_QK
    w_in_t = jnp.swapaxes(w_in, 1, 2)
    c1 = A_COLS + B_COLS + C_COLS

    x2d = x.reshape(m, d)
    for l in range(depth):
        wt = w_in_t[l]
        w_p = jnp.concatenate([wt[A_COLS:c1], wt[:A_COLS],
                               jnp.zeros((n_proj - c1, d), F32)], axis=0).astype(BF16)
        w_g = wt[c1:].astype(BF16)
        p, xn = _proj(x2d, attn_norm[l][None, :], w_p, tm_big, tn_proj)

        wq = jnp.pad(w_uq[l].reshape(MLA_Q_RANK, MLA_HEADS, MLA_QK), ((0, 0), (0, 0), (0, qk_pad))
                     ).reshape(MLA_Q_RANK, MLA_HEADS * MLA_QK_PAD).astype(BF16)
        wkv = w_ukv[l].reshape(MLA_KV_RANK, MLA_HEADS, MLA_NOPE + MLA_V)
        wk = wkv[:, :, :MLA_NOPE].reshape(MLA_KV_RANK, MLA_HEADS * MLA_NOPE).astype(BF16)
        wv = wkv[:, :, MLA_NOPE:].reshape(MLA_KV_RANK, MLA_HEADS * MLA_V).astype(BF16)
        gqh = jnp.pad(mla_q_head_norm[l] * (LOG2_E * MLA_QK ** -0.5), (0, qk_pad))[None, :]
        gkn = mla_k_head_norm[l][None, :MLA_NOPE]
        gkr = jnp.pad(mla_k_head_norm[l][MLA_NOPE:], (0, LANES - MLA_ROPE))[None, :]
        qa, ka, va = _mla_prep(p, off_aq, off_akv, off_ape, wq, wk, wv,
                               mla_q_lat_norm[l][None, :], mla_kv_lat_norm[l][None, :],
                               gqh, gkn, gkr, rope_mla, seq, tm_mla)
        oa = _attention(qa, ka, va, 0, 0, 0, MLA_HEADS, 1, MLA_QK_PAD, MLA_V, batch, seq)

        scale = HEAD_DIM ** -0.5
        gq = jnp.tile(gqa_q_norm[l] * (LOG2_E * scale), GQA_HEADS)[None, :]
        gk = jnp.tile(gqa_k_norm[l], GQA_KV_HEADS)[None, :]
        qb = _headnorm_rope(p, off_bq, nq_b, gq, rope_gqa, seq, tm_seq)
        kb = _headnorm_rope(p, off_bk, nk_b, gk, rope_gqa, seq, tm_seq)
        ob = _attention(qb, kb, p, 0, 0, off_bv, GQA_HEADS, GQA_GROUP, HEAD_DIM, HEAD_DIM,
                        batch, seq)

        oc = _dilated(p, off_cq, off_ck, off_cv, (dil_q_norm[l] * scale)[None, :],
                      dil_k_norm[l][None, :], rope_dil, batch, seq)

        merged = _merge(xn, oa, ob, oc, w_g, b_gate[l][None, :], w_oa_b, w_ob_b, w_oc_b, l,
                        tm_big, _largest_tile(d, 512))
        x2d = _resid_matmul(x2d, merged, w_out_b, l, tm_half, d)
        x2d = _mlp(x2d, mlp_norm[l][None, :], w_up_b, w_down_b, l,
                   _largest_tile(m, 1024), _largest_tile(d_ff, 512))
    return x2d.reshape(batch, seq, d)
```

```python
import functools
import math

import numpy as np
import jax
import jax.numpy as jnp
from jax import lax
from jax.experimental import pallas as pl
from jax.experimental.pallas import tpu as pltpu

F32 = jnp.float32
BF16 = jnp.bfloat16

LANES = 128
VMEM_CAP_BYTES = 60000 * 1024

HEAD_DIM = 128
GRID_W = 64
EPS = 1e-6
NEG_INF = -1e30
NT_DIMS = (((1,), (1,)), ((), ()))

MLA_HEADS = 8
MLA_Q_RANK = 512
MLA_KV_RANK = 256
MLA_NOPE = 128
MLA_ROPE = 64
MLA_V = 128
MLA_ROPE_THETA = 10000.0
MLA_QK = MLA_NOPE + MLA_ROPE
MLA_QK_PAD = 2 * LANES

GQA_HEADS = 8
GQA_KV_HEADS = 2
GQA_GROUP = GQA_HEADS // GQA_KV_HEADS
AXIAL_THETA = 10000.0

DIL_PATTERNS = ((128, 1), (512, 4), (2048, 16))
DIL_HPG = 4
DIL_HEADS = DIL_HPG * len(DIL_PATTERNS)
PARTIAL_ROPE_DIM = HEAD_DIM // 4
PARTIAL_ROPE_THETA = 500000.0
QBLOCK = 128
ATTN_SUB_ROWS = 256
LOG2_E = 1.4426950408889634
ATTN_HEADS_PER_STEP = 2
DIL_PREP_ROWS = 256

A_COLS = MLA_Q_RANK + MLA_KV_RANK + MLA_ROPE
B_COLS = (GQA_HEADS + 2 * GQA_KV_HEADS) * HEAD_DIM
C_COLS = 3 * DIL_HEADS * HEAD_DIM


def _round_up(n, m):
    return -(-n // m) * m


def _params(semantics, *block_bytes, scratch_bytes=0):
    need = 2 * sum(block_bytes) + scratch_bytes
    limit = min(VMEM_CAP_BYTES, need + (24 << 20))
    return pltpu.CompilerParams(dimension_semantics=semantics, vmem_limit_bytes=limit)


def _nbytes(shape, dtype):
    n = 1
    for s in shape:
        n *= s
    return n * jnp.dtype(dtype).itemsize


def _rope_tables(seq, dist, segments):
    cos = np.ones((seq, LANES), np.float64)
    sin_x1 = np.zeros((seq, LANES), np.float64)
    sin_x2 = np.zeros((seq, LANES), np.float64)
    for first, half, pos, theta in segments:
        inv = theta ** (-np.arange(half, dtype=np.float64) / half)
        ang = pos[:, None] * inv[None, :]
        c, s = np.cos(ang), np.sin(ang)
        hi = first + dist
        cos[:, first:first + half] = c
        cos[:, hi:hi + half] = c
        sin_x1[:, first:first + half] = -s
        sin_x2[:, hi:hi + half] = s
    as_table = lambda t: jnp.asarray(t.astype(np.float32))
    if 2 * dist == LANES:
        return (as_table(cos), as_table(sin_x1 + sin_x2)), (dist,)
    return (as_table(cos), as_table(sin_x1), as_table(sin_x2)), (LANES - dist, dist)


def _rope(x, tables, shifts):
    out = x * tables[0]
    for table, shift in zip(tables[1:], shifts):
        out = out + pltpu.roll(x, shift, 1) * table
    return out


def _proj_kernel(x_ref, g_ref, w_ref, o_ref, xn_ref):
    @pl.when(pl.program_id(1) == 0)
    def _():
        x = x_ref[...]
        ms = jnp.mean(x * x, axis=-1, keepdims=True)
        xn_ref[...] = (x * lax.rsqrt(ms + EPS) * g_ref[...]).astype(BF16)

    o_ref[...] = lax.dot_general(xn_ref[...], w_ref[...], NT_DIMS,
                                 preferred_element_type=F32).astype(o_ref.dtype)


def _proj(x2d, gain, w_t, tm, tn):
    m, d = x2d.shape
    n = w_t.shape[0]
    return pl.pallas_call(
        _proj_kernel,
        grid=(m // tm, n // tn),
        in_specs=[
            pl.BlockSpec((tm, d), lambda i, j: (i, 0)),
            pl.BlockSpec((1, d), lambda i, j: (0, 0)),
            pl.BlockSpec((tn, d), lambda i, j: (j, 0)),
        ],
        out_specs=[pl.BlockSpec((tm, tn), lambda i, j: (i, j)),
                   pl.BlockSpec((tm, d), lambda i, j: (i, 0))],
        out_shape=[jax.ShapeDtypeStruct((m, n), BF16), jax.ShapeDtypeStruct((m, d), BF16)],
        compiler_params=_params(
            ("parallel", "arbitrary"),
            _nbytes((tm, d), F32), _nbytes((d, tn), BF16), _nbytes((tm, tn), BF16),
            _nbytes((tm, d), BF16), scratch_bytes=_nbytes((tm, d), F32)),
        name="proj",
    )(x2d, gain, w_t)


def _gather_rows_kernel(w_ref, o_ref, *, valid_rows):
    r = pl.program_id(0)
    limit = functools.reduce(lambda acc, t: jnp.where(r == t[0], t[1], acc),
                             enumerate(valid_rows), o_ref.shape[0])
    row = lax.broadcasted_iota(jnp.int32, o_ref.shape, 0)
    o_ref[...] = jnp.where(row < limit, w_ref[...], 0.0).astype(o_ref.dtype)


def _gather_rows(w_t, layer, starts, valid_rows, tr):
    _, n, d = w_t.shape
    assert all(s % 8 == 0 and s + tr <= n for s in starts)
    kern = functools.partial(_gather_rows_kernel, valid_rows=tuple(valid_rows))

    def start_of(r):
        start = functools.reduce(lambda acc, t: jnp.where(r == t[0], t[1], acc), enumerate(starts), 0)
        return pl.multiple_of(start, 8)

    return pl.pallas_call(
        kern,
        grid=(len(starts),),
        in_specs=[pl.BlockSpec((None, pl.Element(tr), pl.Element(d)),
                               lambda r: (layer, start_of(r), 0))],
        out_specs=pl.BlockSpec((tr, d), lambda r: (r, 0)),
        out_shape=jax.ShapeDtypeStruct((len(starts) * tr, d), BF16),
        compiler_params=_params(("parallel",), _nbytes((tr, d), F32), _nbytes((tr, d), BF16)),
        name="gather_rows",
    )(w_t)


def _row_meansq(x):
    n = x.shape[-1]
    assert n & (n - 1) == 0
    sq = x * x
    hi = sq.astype(BF16)
    lo = (sq - hi.astype(F32)).astype(BF16)
    weights = jnp.full((n, LANES), 1.0 / n, BF16)
    return (jnp.dot(hi, weights, preferred_element_type=F32)
            + jnp.dot(lo, weights, preferred_element_type=F32))


def _headnorm_rope_kernel(x_ref, g_ref, *rest, nheads, shifts):
    table_refs, o_ref = rest[:-1], rest[-1]
    tables = [t[...] for t in table_refs]
    for h in range(nheads):
        cols = slice(h * HEAD_DIM, (h + 1) * HEAD_DIM)
        x = x_ref[:, cols].astype(F32)
        ms = _row_meansq(x)
        y = x * lax.rsqrt(ms + EPS) * g_ref[:, cols]
        o_ref[:, cols] = _rope(y, tables, shifts).astype(o_ref.dtype)


def _headnorm_rope(p, col_off, width, gains, rope, seq, tm):
    m = p.shape[0]
    assert col_off % width == 0
    tables, shifts = rope
    cb = col_off // width
    sb = seq // tm
    kern = functools.partial(_headnorm_rope_kernel, nheads=width // HEAD_DIM, shifts=shifts)
    tab_spec = pl.BlockSpec((tm, LANES), lambda i: (i % sb, 0))
    return pl.pallas_call(
        kern,
        grid=(m // tm,),
        in_specs=[
            pl.BlockSpec((tm, width), lambda i: (i, cb)),
            pl.BlockSpec((1, width), lambda i: (0, 0)),
        ] + [tab_spec] * len(tables),
        out_specs=pl.BlockSpec((tm, width), lambda i: (i, 0)),
        out_shape=jax.ShapeDtypeStruct((m, width), BF16),
        compiler_params=_params(
            ("parallel",),
            2 * _nbytes((tm, width), BF16), len(tables) * _nbytes((tm, LANES), F32),
            scratch_bytes=4 * _nbytes((tm, LANES), F32)),
        name="headnorm_rope",
    )(p, gains, *tables)


def _mla_prep_kernel(cq_ref, ckv_ref, kpe_ref, wq_ref, wk_ref, wv_ref, gql_ref, gkl_ref,
                     gqh_ref, gkn_ref, gkr_ref, *rest, shifts):
    tables = [t[...] for t in rest[:-3]]
    q_ref, k_ref, v_ref = rest[-3:]
    sumsq = lambda t: jnp.sum(t * t, axis=-1, keepdims=True)

    cq = cq_ref[...].astype(F32)
    cqn = cq * lax.rsqrt(sumsq(cq) * (1.0 / MLA_Q_RANK) + EPS) * gql_ref[...]
    q = jnp.dot(cqn.astype(BF16), wq_ref[...], preferred_element_type=F32)
    gqh = gqh_ref[...]
    for h in range(MLA_HEADS):
        nope = slice(h * MLA_QK_PAD, h * MLA_QK_PAD + LANES)
        rope = slice(h * MLA_QK_PAD + LANES, (h + 1) * MLA_QK_PAD)
        inv = lax.rsqrt(sumsq(q[:, h * MLA_QK_PAD:(h + 1) * MLA_QK_PAD]) * (1.0 / MLA_QK) + EPS)
        q_ref[:, nope] = (q[:, nope] * inv * gqh[:, :LANES]).astype(q_ref.dtype)
        q_ref[:, rope] = _rope(q[:, rope] * inv * gqh[:, LANES:], tables, shifts).astype(q_ref.dtype)

    ckv = ckv_ref[...].astype(F32)
    ckvn = (ckv * lax.rsqrt(sumsq(ckv) * (1.0 / MLA_KV_RANK) + EPS) * gkl_ref[...]).astype(BF16)
    kn = jnp.dot(ckvn, wk_ref[...], preferred_element_type=F32)
    v_ref[...] = jnp.dot(ckvn, wv_ref[...], preferred_element_type=F32).astype(v_ref.dtype)
    kpe = kpe_ref[...].astype(F32)
    pe_sq = sumsq(kpe)
    gkn = gkn_ref[...]
    kpe_rot = _rope(kpe * gkr_ref[...], tables, shifts)
    for h in range(MLA_HEADS):
        kh = kn[:, h * MLA_NOPE:(h + 1) * MLA_NOPE]
        inv = lax.rsqrt((sumsq(kh) + pe_sq) * (1.0 / MLA_QK) + EPS)
        k_ref[:, h * MLA_QK_PAD:h * MLA_QK_PAD + LANES] = (kh * inv * gkn).astype(k_ref.dtype)
        k_ref[:, h * MLA_QK_PAD + LANES:(h + 1) * MLA_QK_PAD] = (kpe_rot * inv).astype(k_ref.dtype)


def _mla_prep(p, off_cq, off_ckv, off_kpe, wq, wk, wv, gql, gkl, gqh, gkn, gkr, rope, seq, tm):
    m = p.shape[0]
    tables, shifts = rope
    sb = seq // tm
    const = lambda shape: pl.BlockSpec(shape, lambda i: (0, 0))
    tab_spec = pl.BlockSpec((tm, LANES), lambda i: (i % sb, 0))
    qk_cols = MLA_HEADS * MLA_QK_PAD
    v_cols = MLA_HEADS * MLA_V
    return pl.pallas_call(
        functools.partial(_mla_prep_kernel, shifts=shifts),
        grid=(m // tm,),
        in_specs=[
            pl.BlockSpec((tm, MLA_Q_RANK), lambda i: (i, off_cq // MLA_Q_RANK)),
            pl.BlockSpec((tm, MLA_KV_RANK), lambda i: (i, off_ckv // MLA_KV_RANK)),
            pl.BlockSpec((tm, LANES), lambda i: (i, off_kpe // LANES)),
            const(wq.shape), const(wk.shape), const(wv.shape),
            const(gql.shape), const(gkl.shape), const(gqh.shape), const(gkn.shape), const(gkr.shape),
        ] + [tab_spec] * len(tables),
        out_specs=[
            pl.BlockSpec((tm, qk_cols), lambda i: (i, 0)),
            pl.BlockSpec((tm, qk_cols), lambda i: (i, 0)),
            pl.BlockSpec((tm, v_cols), lambda i: (i, 0)),
        ],
        out_shape=[
            jax.ShapeDtypeStruct((m, qk_cols), BF16),
            jax.ShapeDtypeStruct((m, qk_cols), BF16),
            jax.ShapeDtypeStruct((m, v_cols), BF16),
        ],
        compiler_params=_params(
            ("parallel",),
            _nbytes((tm, MLA_Q_RANK + MLA_KV_RANK + LANES), BF16),
            _nbytes(wq.shape, BF16), _nbytes(wk.shape, BF16), _nbytes(wv.shape, BF16),
            _nbytes((tm, 2 * qk_cols + v_cols), BF16), len(tables) * _nbytes((tm, LANES), F32),
            scratch_bytes=3 * _nbytes((tm, qk_cols), F32)),
        name="mla_prep",
    )(p, p, p, wq, wk, wv, gql, gkl, gqh, gkn, gkr, *tables)


def _attn_kernel(q_ref, k_ref, v_ref, o_ref, *, sub, heads, dk, dv, shared_kv):
    for h in range(heads):
        kv = 0 if shared_kv else h
        k = k_ref[:, kv * dk:(kv + 1) * dk]
        v = v_ref[:, kv * dv:(kv + 1) * dv]
        v_ones = jnp.concatenate([v, jnp.ones_like(v)], axis=1)
        for r0 in range(0, q_ref.shape[0], sub):
            rows = slice(r0, r0 + sub)
            s = lax.dot_general(q_ref[rows, h * dk:(h + 1) * dk], k, (((1,), (1,)), ((), ())),
                                preferred_element_type=F32)
            p = jnp.exp2(s - jnp.max(s, axis=-1, keepdims=True)).astype(BF16)
            ov = jnp.dot(p, v_ones, preferred_element_type=F32)
            o_ref[rows, h * dv:(h + 1) * dv] = (ov[:, :dv] * (1.0 / ov[:, dv:])).astype(o_ref.dtype)


def _attention(q_arr, k_arr, v_arr, q_off, k_off, v_off, nheads, group, dk, dv, batch, seq):
    hps = ATTN_HEADS_PER_STEP
    shared_kv = group % hps == 0
    assert nheads % hps == 0 and (shared_kv or group == 1)
    kv_w = 1 if shared_kv else hps
    assert q_off % (hps * dk) == 0 and k_off % (kv_w * dk) == 0 and v_off % (kv_w * dv) == 0
    qb, kb, vb = q_off // (hps * dk), k_off // (kv_w * dk), v_off // (kv_w * dv)
    kv_of = (lambda j: (j * hps) // group) if shared_kv else (lambda j: j)
    sub = min(seq, ATTN_SUB_ROWS)
    kern = functools.partial(_attn_kernel, sub=sub, heads=hps, dk=dk, dv=dv, shared_kv=shared_kv)
    return pl.pallas_call(
        kern,
        grid=(batch, nheads // hps),
        in_specs=[
            pl.BlockSpec((seq, hps * dk), lambda b, j: (b, qb + j)),
            pl.BlockSpec((seq, kv_w * dk), lambda b, j: (b, kb + kv_of(j))),
            pl.BlockSpec((seq, kv_w * dv), lambda b, j: (b, vb + kv_of(j))),
        ],
        out_specs=pl.BlockSpec((seq, hps * dv), lambda b, j: (b, j)),
        out_shape=jax.ShapeDtypeStruct((batch * seq, nheads * dv), BF16),
        compiler_params=_params(
            ("parallel", "parallel"),
            _nbytes((seq, hps * dk), BF16), _nbytes((seq, kv_w * dk), BF16),
            _nbytes((seq, kv_w * dv), BF16), _nbytes((seq, hps * dv), BF16),
            scratch_bytes=4 * _nbytes((sub, seq), F32)),
        name="attention",
    )(q_arr, k_arr, v_arr)


def _dilated_kernel(q0_ref, q1_ref, q2_ref, k0_ref, k1_ref, k2_ref, v0_ref, v1_ref, v2_ref,
                    gq_ref, gk_ref, *rest, seq, patterns, shifts):
    n_tab = len(shifts) + 1
    table_refs, out_ref = rest[:n_tab], rest[n_tab]
    qb_ref, kb_ref, qf_ref, kf_ref, vf_ref, of_ref, lf_ref, s_ref, p_ref, m_ref = rest[n_tab + 1:]
    q_refs, k_refs, v_refs = (q0_ref, q1_ref, q2_ref), (k0_ref, k1_ref, k2_ref), (v0_ref, v1_ref, v2_ref)
    strided = [g for g, (_, dil) in enumerate(patterns) if dil > 1]
    dense = [g for g, (_, dil) in enumerate(patterns) if dil == 1]

    def prepared(src, gain, rows):
        x = src[rows, :].astype(F32)
        ms = _row_meansq(x)
        return _rope(x * lax.rsqrt(ms + EPS) * gain[...], [t[rows, :] for t in table_refs], shifts)

    for g, (window, dil) in enumerate(patterns):
        radius = window // (2 * dil)
        length = seq // dil
        kw = min(length, QBLOCK + 2 * radius)
        slot = strided.index(g) if dil > 1 else dense.index(g)
        for r0 in range(0, seq, DIL_PREP_ROWS):
            chunk = slice(r0, r0 + DIL_PREP_ROWS)
            if dil > 1:
                qf_ref[slot, chunk, :] = prepared(q_refs[g], gq_ref, chunk)
                kf_ref[slot, chunk, :] = prepared(k_refs[g], gk_ref, chunk)
                vf_ref[slot, chunk, :] = v_refs[g][chunk, :].astype(F32)
            else:
                qb_ref[slot, chunk, :] = prepared(q_refs[g], gq_ref, chunk).astype(BF16)
                kb_ref[slot, chunk, :] = prepared(k_refs[g], gk_ref, chunk).astype(BF16)
        def rows(r, start, n, dil=dil):
            return pl.ds(start, n) if dil == 1 else pl.ds(start * dil + r, n, stride=dil)

        def window_of(dense_ref, copy, r, start, n, dil=dil, slot=slot, rows=rows):
            if dil == 1:
                return dense_ref[rows(r, start, n), :]
            return copy[slot, rows(r, start, n), :].astype(BF16)

        tiles = [(r, t * QBLOCK, min(max(t * QBLOCK - radius, 0), length - kw))
                 for r in range(dil) for t in range(length // QBLOCK)]
        for i, (r, q0, ks) in enumerate(tiles):
            q = window_of(qb_ref.at[slot] if dil == 1 else None, qf_ref, r, q0, QBLOCK)
            k = window_of(kb_ref.at[slot] if dil == 1 else None, kf_ref, r, ks, kw)
            s = lax.dot_general(q, k, (((1,), (1,)), ((), ())), preferred_element_type=F32)
            rel = (lax.broadcasted_iota(jnp.int32, s.shape, 0)
                   - lax.broadcasted_iota(jnp.int32, s.shape, 1)) + (q0 - ks)
            s_ref[i, :, :kw] = jnp.where(jnp.abs(rel) <= radius, s, NEG_INF)
        for i in range(len(tiles)):
            s = s_ref[i, :, :kw]
            m = jnp.max(s, axis=-1, keepdims=True)
            p_ref[i, :, :kw] = jnp.exp(s - m).astype(BF16)
            m_ref[i] = jnp.broadcast_to(m, (QBLOCK, HEAD_DIM))
        for i, (r, q0, ks) in enumerate(tiles):
            v = window_of(v_refs[g], vf_ref, r, ks, kw)
            ov = jnp.dot(p_ref[i, :, :kw], jnp.concatenate([v, jnp.ones_like(v)], axis=1),
                         preferred_element_type=F32)
            l = ov[:, HEAD_DIM:]
            of_ref[g, rows(r, q0, QBLOCK), :] = ov[:, :HEAD_DIM] * (1.0 / l)
            lf_ref[g, rows(r, q0, QBLOCK), :] = m_ref[i] + jnp.log(l)

    la, lb, lc = lf_ref[0], lf_ref[1], lf_ref[2]
    m = jnp.maximum(jnp.maximum(la, lb), lc)
    ea, eb, ec = jnp.exp(la - m), jnp.exp(lb - m), jnp.exp(lc - m)
    num = ea * of_ref[0] + eb * of_ref[1] + ec * of_ref[2]
    out_ref[...] = (num * (1.0 / (ea + eb + ec))).astype(out_ref.dtype)


def _dilated(p, off_q, off_k, off_v, gq, gk, rope, batch, seq):
    ng = len(DIL_PATTERNS)
    n_strided = sum(dil > 1 for _, dil in DIL_PATTERNS)
    tables, shifts = rope
    kw_max = max(min(seq // dil, QBLOCK + window // dil) for window, dil in DIL_PATTERNS)
    kern = functools.partial(_dilated_kernel, seq=seq, patterns=DIL_PATTERNS, shifts=shifts)
    group_w = DIL_HPG * HEAD_DIM

    def head(off, g):
        assert off % HEAD_DIM == 0
        cb = (off + g * group_w) // HEAD_DIM
        return pl.BlockSpec((seq, HEAD_DIM), lambda b, h: (b, cb + h))

    const = lambda shape: pl.BlockSpec(shape, lambda b, h: (0, 0))
    slab = _nbytes((seq, HEAD_DIM), F32)
    return pl.pallas_call(
        kern,
        grid=(batch, DIL_HPG),
        in_specs=([head(off_q, g) for g in range(ng)] + [head(off_k, g) for g in range(ng)]
                  + [head(off_v, g) for g in range(ng)]
                  + [const((1, HEAD_DIM))] * 2 + [const((seq, LANES))] * len(tables)),
        out_specs=pl.BlockSpec((seq, HEAD_DIM), lambda b, h: (b, h)),
        out_shape=jax.ShapeDtypeStruct((batch * seq, group_w), BF16),
        scratch_shapes=[pltpu.VMEM((ng - n_strided, seq, HEAD_DIM), BF16)] * 2
                       + [pltpu.VMEM((n_strided, seq, HEAD_DIM), F32)] * 3
                       + [pltpu.VMEM((ng, seq, HEAD_DIM), F32)] * 2
                       + [pltpu.VMEM((seq // QBLOCK, QBLOCK, kw_max), F32),
                          pltpu.VMEM((seq // QBLOCK, QBLOCK, kw_max), BF16),
                          pltpu.VMEM((seq // QBLOCK, QBLOCK, HEAD_DIM), F32)],
        compiler_params=_params(
            ("parallel", "parallel"),
            (3 * ng + 1) * _nbytes((seq, HEAD_DIM), BF16), len(tables) * slab,
            scratch_bytes=(ng - n_strided + 3 * n_strided + 2 * ng) * slab + 4 * slab),
        name="dilated",
    )(*([p] * (3 * ng)), gq, gk, *tables)


def _merge_kernel(xn_ref, oa_ref, ob_ref, oc_ref, wga_ref, wgb_ref, wgc_ref, bga_ref, bgb_ref, bgc_ref,
                  wa_ref, wb_ref, wc_ref, o_ref):
    xn = xn_ref[...]
    merged = None
    for o_i, w_i, wg_i, bg_i in ((oa_ref, wa_ref, wga_ref, bga_ref), (ob_ref, wb_ref, wgb_ref, bgb_ref),
                                 (oc_ref, wc_ref, wgc_ref, bgc_ref)):
        logit = lax.dot_general(xn, wg_i[...], NT_DIMS, preferred_element_type=F32) + bg_i[...]
        gate = 1.0 / (1.0 + jnp.exp(-logit))
        term = gate * jnp.dot(o_i[...], w_i[...], preferred_element_type=F32)
        merged = term if merged is None else merged + term
    o_ref[...] = merged.astype(o_ref.dtype)


def _merge(xn, oa, ob, oc, w_gate, b_gate, w_oa, w_ob, w_oc, layer, tm, tn):
    m, d = xn.shape
    nj = d // tn
    row = lambda a: pl.BlockSpec((tm, a.shape[1]), lambda i, j: (i, 0))
    wcol = lambda w: pl.BlockSpec((None, w.shape[1], tn), lambda i, j: (layer, 0, j))
    wgate = lambda g: pl.BlockSpec((tn, d), lambda i, j: (g * nj + j, 0))
    bgate = lambda g: pl.BlockSpec((1, tn), lambda i, j: (0, g * nj + j))
    k_branches = oa.shape[1] + ob.shape[1] + oc.shape[1]
    return pl.pallas_call(
        _merge_kernel,
        grid=(m // tm, nj),
        in_specs=[row(xn), row(oa), row(ob), row(oc), wgate(0), wgate(1), wgate(2),
                  bgate(0), bgate(1), bgate(2), wcol(w_oa), wcol(w_ob), wcol(w_oc)],
        out_specs=pl.BlockSpec((tm, tn), lambda i, j: (i, j)),
        out_shape=jax.ShapeDtypeStruct((m, d), BF16),
        compiler_params=_params(
            ("parallel", "parallel"),
            _nbytes((tm, d + k_branches), BF16), _nbytes((tm, tn), BF16),
            _nbytes((3 * d + k_branches, tn), BF16),
            scratch_bytes=6 * _nbytes((tm, tn), F32)),
        name="merge",
    )(xn, oa, ob, oc, w_gate, w_gate, w_gate, b_gate, b_gate, b_gate, w_oa, w_ob, w_oc)


def _resid_matmul_kernel(x_ref, a_ref, w_ref, o_ref):
    o_ref[...] = x_ref[...] + jnp.dot(a_ref[...], w_ref[...], preferred_element_type=F32)


def _resid_matmul(x2d, a, w, layer, tm, tn):
    m, d = x2d.shape
    k = a.shape[1]
    return pl.pallas_call(
        _resid_matmul_kernel,
        grid=(m // tm, d // tn),
        in_specs=[
            pl.BlockSpec((tm, tn), lambda i, j: (i, j)),
            pl.BlockSpec((tm, k), lambda i, j: (i, 0)),
            pl.BlockSpec((None, k, tn), lambda i, j: (layer, 0, j)),
        ],
        out_specs=pl.BlockSpec((tm, tn), lambda i, j: (i, j)),
        out_shape=jax.ShapeDtypeStruct((m, d), F32),
        compiler_params=_params(
            ("parallel", "parallel"),
            2 * _nbytes((tm, tn), F32), _nbytes((tm, k), BF16), _nbytes((k, tn), BF16)),
        name="out_proj",
    )(x2d, a, w)


def _mlp_kernel(x_ref, g_ref, wu_ref, wd_ref, o_ref, xn_ref):
    c = pl.program_id(1)

    @pl.when(c == 0)
    def _():
        x = x_ref[...]
        ms = jnp.mean(x * x, axis=-1, keepdims=True)
        xn_ref[...] = (x * lax.rsqrt(ms + EPS) * g_ref[...]).astype(BF16)
        o_ref[...] = x

    h = jnp.dot(xn_ref[...], wu_ref[...], preferred_element_type=F32)
    h = jnp.square(jnp.maximum(h, 0.0)).astype(BF16)
    o_ref[...] += jnp.dot(h, wd_ref[...], preferred_element_type=F32)


def _mlp(x2d, gain, w_up, w_down, layer, tm, tf):
    m, d = x2d.shape
    f = w_up.shape[2]
    return pl.pallas_call(
        _mlp_kernel,
        grid=(m // tm, f // tf),
        in_specs=[
            pl.BlockSpec((tm, d), lambda i, c: (i, 0)),
            pl.BlockSpec((1, d), lambda i, c: (0, 0)),
            pl.BlockSpec((None, d, tf), lambda i, c: (layer, 0, c)),
            pl.BlockSpec((None, tf, d), lambda i, c: (layer, c, 0)),
        ],
        out_specs=pl.BlockSpec((tm, d), lambda i, c: (i, 0)),
        out_shape=jax.ShapeDtypeStruct((m, d), F32),
        scratch_shapes=[pltpu.VMEM((tm, d), BF16)],
        compiler_params=_params(
            ("parallel", "arbitrary"),
            2 * _nbytes((tm, d), F32), 2 * _nbytes((d, tf), BF16),
            scratch_bytes=_nbytes((tm, d), BF16) + _nbytes((tm, tf), F32)),
        name="mlp",
    )(x2d, gain, w_up, w_down)


def _largest_tile(n, cap):
    t = min(n, cap)
    while n % t:
        t //= 2
    return t


def kernel(x, attn_norm, w_in, b_gate, mla_q_lat_norm, w_uq, mla_kv_lat_norm, w_ukv,
           mla_q_head_norm, mla_k_head_norm, gqa_q_norm, gqa_k_norm, dil_q_norm, dil_k_norm,
           w_oa, w_ob, w_oc, w_out, mlp_norm, w_up, w_down):
    batch, seq, d = x.shape
    depth = w_in.shape[0]
    m = batch * seq
    d_ff = w_up.shape[2]
    gate_cols = 3 * d
    assert w_in.shape[2] == A_COLS + B_COLS + C_COLS + gate_cols
    assert seq % GRID_W == 0 and seq % (16 * QBLOCK) == 0 and d % LANES == 0

    tn_proj = 1792
    nq_b, nk_b = GQA_HEADS * HEAD_DIM, GQA_KV_HEADS * HEAD_DIM
    nh_c = DIL_HEADS * HEAD_DIM
    off_bq = 0
    off_bk = off_bq + nq_b
    off_bv = off_bk + nk_b
    off_cq = off_bv + nk_b
    off_ck = off_cq + nh_c
    off_cv = off_ck + nh_c
    off_aq = off_cv + nh_c
    off_akv = off_aq + MLA_Q_RANK
    off_ape = off_akv + MLA_KV_RANK
    n_proj = _round_up(off_ape + LANES, tn_proj)

    tm_big = _largest_tile(m, 1024)
    tm_half = _largest_tile(m, 512)
    tm_seq = _largest_tile(seq, 1024)
    tm_mla = _largest_tile(seq, 512)

    pos = np.arange(seq, dtype=np.float64)
    row = np.repeat(np.arange(seq // GRID_W, dtype=np.float64), GRID_W)
    col = np.tile(np.arange(GRID_W, dtype=np.float64), seq // GRID_W)
    half_mla, half_ax, half_dil = MLA_ROPE // 2, HEAD_DIM // 4, PARTIAL_ROPE_DIM // 2
    rope_mla = _rope_tables(seq, half_mla, [(0, half_mla, pos, MLA_ROPE_THETA)])
    rope_gqa = _rope_tables(seq, half_ax, [(0, half_ax, row, AXIAL_THETA),
                                           (2 * half_ax, half_ax, col, AXIAL_THETA)])
    rope_dil = _rope_tables(seq, half_dil, [(0, half_dil, pos, PARTIAL_ROPE_THETA)])

    w_oa_b, w_ob_b, w_oc_b, w_out_b = (w.astype(BF16) for w in (w_oa, w_ob, w_oc, w_out))
    w_up_b, w_down_b = w_up.astype(BF16), w_down.astype(BF16)
    qk_pad = MLA_QK_PAD - MLA_QK
    w_in_t = jnp.swapaxes(w_in, 1, 2)
    c1 = A_COLS + B_COLS + C_COLS
    tr_w = math.gcd(c1 - A_COLS, gate_cols, 1024)
    a_blocks = -(-A_COLS // tr_w)
    starts_p = [A_COLS + r * tr_w for r in range((c1 - A_COLS) // tr_w)] + [k * tr_w for k in range(a_blocks)]
    valid_p = ([tr_w] * ((c1 - A_COLS) // tr_w)
               + [min(tr_w, A_COLS - k * tr_w) for k in range(a_blocks)])
    starts_g = [c1 + r * tr_w for r in range(gate_cols // tr_w)]
    assert len(starts_p) * tr_w == n_proj

    x2d = x.reshape(m, d)
    for l in range(depth):
        w_p = _gather_rows(w_in_t, l, starts_p, valid_p, tr_w)
        w_g = _gather_rows(w_in_t, l, starts_g, [tr_w] * len(starts_g), tr_w)
        p, xn = _proj(x2d, attn_norm[l][None, :], w_p, tm_big, tn_proj)

        wq = jnp.pad(w_uq[l].reshape(MLA_Q_RANK, MLA_HEADS, MLA_QK), ((0, 0), (0, 0), (0, qk_pad))
                     ).reshape(MLA_Q_RANK, MLA_HEADS * MLA_QK_PAD).astype(BF16)
        wkv = w_ukv[l].reshape(MLA_KV_RANK, MLA_HEADS, MLA_NOPE + MLA_V)
        wk = wkv[:, :, :MLA_NOPE].reshape(MLA_KV_RANK, MLA_HEADS * MLA_NOPE).astype(BF16)
        wv = wkv[:, :, MLA_NOPE:].reshape(MLA_KV_RANK, MLA_HEADS * MLA_V).astype(BF16)
        gqh = jnp.pad(mla_q_head_norm[l] * (LOG2_E * MLA_QK ** -0.5), (0, qk_pad))[None, :]
        gkn = mla_k_head_norm[l][None, :MLA_NOPE]
        gkr = jnp.pad(mla_k_head_norm[l][MLA_NOPE:], (0, LANES - MLA_ROPE))[None, :]
        qa, ka, va = _mla_prep(p, off_aq, off_akv, off_ape, wq, wk, wv,
                               mla_q_lat_norm[l][None, :], mla_kv_lat_norm[l][None, :],
                               gqh, gkn, gkr, rope_mla, seq, tm_mla)
        oa = _attention(qa, ka, va, 0, 0, 0, MLA_HEADS, 1, MLA_QK_PAD, MLA_V, batch, seq)

        scale = HEAD_DIM ** -0.5
        gq = jnp.tile(gqa_q_norm[l] * (LOG2_E * scale), GQA_HEADS)[None, :]
        gk = jnp.tile(gqa_k_norm[l], GQA_KV_HEADS)[None, :]
        qb = _headnorm_rope(p, off_bq, nq_b, gq, rope_gqa, seq, tm_seq)
        kb = _headnorm_rope(p, off_bk, nk_b, gk, rope_gqa, seq, tm_seq)
        ob = _attention(qb, kb, p, 0, 0, off_bv, GQA_HEADS, GQA_GROUP, HEAD_DIM, HEAD_DIM,
                        batch, seq)

        oc = _dilated(p, off_cq, off_ck, off_cv, (dil_q_norm[l] * scale)[None, :],
                      dil_k_norm[l][None, :], rope_dil, batch, seq)

        merged = _merge(xn, oa, ob, oc, w_g, b_gate[l][None, :], w_oa_b, w_ob_b, w_oc_b, l,
                        tm_big, _largest_tile(d, 512))
        x2d = _resid_matmul(x2d, merged, w_out_b, l, tm_half, d)
        x2d = _mlp(x2d, mlp_norm[l][None, :], w_up_b, w_down_b, l,
                   _largest_tile(m, 1024), _largest_tile(d_ff, 512))
    return x2d.reshape(batch, seq, d)
```

```python
import functools
import math

import numpy as np
import jax
import jax.numpy as jnp
from jax import lax
from jax.experimental import pallas as pl
from jax.experimental.pallas import tpu as pltpu

F32 = jnp.float32
BF16 = jnp.bfloat16

LANES = 128
VMEM_CAP_BYTES = 60000 * 1024

HEAD_DIM = 128
GRID_W = 64
EPS = 1e-6
NEG_INF = -1e30
NT_DIMS = (((1,), (1,)), ((), ()))

MLA_HEADS = 8
MLA_Q_RANK = 512
MLA_KV_RANK = 256
MLA_NOPE = 128
MLA_ROPE = 64
MLA_V = 128
MLA_ROPE_THETA = 10000.0
MLA_QK = MLA_NOPE + MLA_ROPE
MLA_QK_PAD = 2 * LANES

GQA_HEADS = 8
GQA_KV_HEADS = 2
GQA_GROUP = GQA_HEADS // GQA_KV_HEADS
AXIAL_THETA = 10000.0

DIL_PATTERNS = ((128, 1), (512, 4), (2048, 16))
DIL_HPG = 4
DIL_HEADS = DIL_HPG * len(DIL_PATTERNS)
PARTIAL_ROPE_DIM = HEAD_DIM // 4
PARTIAL_ROPE_THETA = 500000.0
QBLOCK = 128
ATTN_SUB_ROWS = 256
LOG2_E = 1.4426950408889634
ATTN_HEADS_PER_STEP = 2
DIL_PREP_ROWS = 256

A_COLS = MLA_Q_RANK + MLA_KV_RANK + MLA_ROPE
B_COLS = (GQA_HEADS + 2 * GQA_KV_HEADS) * HEAD_DIM
C_COLS = 3 * DIL_HEADS * HEAD_DIM


def _round_up(n, m):
    return -(-n // m) * m


def _params(semantics, *block_bytes, scratch_bytes=0):
    need = 2 * sum(block_bytes) + scratch_bytes
    limit = min(VMEM_CAP_BYTES, need + (24 << 20))
    return pltpu.CompilerParams(dimension_semantics=semantics, vmem_limit_bytes=limit)


def _nbytes(shape, dtype):
    n = 1
    for s in shape:
        n *= s
    return n * jnp.dtype(dtype).itemsize


def _rope_tables(seq, dist, segments):
    cos = np.ones((seq, LANES), np.float64)
    sin_x1 = np.zeros((seq, LANES), np.float64)
    sin_x2 = np.zeros((seq, LANES), np.float64)
    for first, half, pos, theta in segments:
        inv = theta ** (-np.arange(half, dtype=np.float64) / half)
        ang = pos[:, None] * inv[None, :]
        c, s = np.cos(ang), np.sin(ang)
        hi = first + dist
        cos[:, first:first + half] = c
        cos[:, hi:hi + half] = c
        sin_x1[:, first:first + half] = -s
        sin_x2[:, hi:hi + half] = s
    as_table = lambda t: jnp.asarray(t.astype(np.float32))
    if 2 * dist == LANES:
        return (as_table(cos), as_table(sin_x1 + sin_x2)), (dist,)
    return (as_table(cos), as_table(sin_x1), as_table(sin_x2)), (LANES - dist, dist)


def _rope(x, tables, shifts):
    out = x * tables[0]
    for table, shift in zip(tables[1:], shifts):
        out = out + pltpu.roll(x, shift, 1) * table
    return out


def _proj_kernel(x_ref, g_ref, w_ref, o_ref, xn_ref):
    @pl.when(pl.program_id(1) == 0)
    def _():
        x = x_ref[...]
        ms = jnp.mean(x * x, axis=-1, keepdims=True)
        xn_ref[...] = (x * lax.rsqrt(ms + EPS) * g_ref[...]).astype(BF16)

    o_ref[...] = lax.dot_general(xn_ref[...], w_ref[...], NT_DIMS,
                                 preferred_element_type=F32).astype(o_ref.dtype)


def _proj(x2d, gain, w_t, tm, tn):
    m, d = x2d.shape
    n = w_t.shape[0]
    return pl.pallas_call(
        _proj_kernel,
        grid=(m // tm, n // tn),
        in_specs=[
            pl.BlockSpec((tm, d), lambda i, j: (i, 0)),
            pl.BlockSpec((1, d), lambda i, j: (0, 0)),
            pl.BlockSpec((tn, d), lambda i, j: (j, 0)),
        ],
        out_specs=[pl.BlockSpec((tm, tn), lambda i, j: (i, j)),
                   pl.BlockSpec((tm, d), lambda i, j: (i, 0))],
        out_shape=[jax.ShapeDtypeStruct((m, n), BF16), jax.ShapeDtypeStruct((m, d), BF16)],
        compiler_params=_params(
            ("parallel", "arbitrary"),
            _nbytes((tm, d), F32), _nbytes((d, tn), BF16), _nbytes((tm, tn), BF16),
            _nbytes((tm, d), BF16), scratch_bytes=_nbytes((tm, d), F32)),
        name="proj",
    )(x2d, gain, w_t)


def _gather_rows_kernel(w_ref, o_ref, *, valid_rows):
    r = pl.program_id(0)
    limit = functools.reduce(lambda acc, t: jnp.where(r == t[0], t[1], acc),
                             enumerate(valid_rows), o_ref.shape[0])
    row = lax.broadcasted_iota(jnp.int32, o_ref.shape, 0)
    o_ref[...] = jnp.where(row < limit, w_ref[...], 0.0).astype(o_ref.dtype)


def _gather_rows(w_t, layer, starts, valid_rows, tr):
    _, n, d = w_t.shape
    assert all(s % 8 == 0 and s + tr <= n for s in starts)
    kern = functools.partial(_gather_rows_kernel, valid_rows=tuple(valid_rows))

    def start_of(r):
        start = functools.reduce(lambda acc, t: jnp.where(r == t[0], t[1], acc), enumerate(starts), 0)
        return pl.multiple_of(start, 8)

    return pl.pallas_call(
        kern,
        grid=(len(starts),),
        in_specs=[pl.BlockSpec((None, pl.Element(tr), pl.Element(d)),
                               lambda r: (layer, start_of(r), 0))],
        out_specs=pl.BlockSpec((tr, d), lambda r: (r, 0)),
        out_shape=jax.ShapeDtypeStruct((len(starts) * tr, d), BF16),
        compiler_params=_params(("parallel",), _nbytes((tr, d), F32), _nbytes((tr, d), BF16)),
        name="gather_rows",
    )(w_t)


def _row_meansq(x):
    n = x.shape[-1]
    assert n & (n - 1) == 0
    sq = x * x
    hi = sq.astype(BF16)
    lo = (sq - hi.astype(F32)).astype(BF16)
    weights = jnp.full((n, LANES), 1.0 / n, BF16)
    return (jnp.dot(hi, weights, preferred_element_type=F32)
            + jnp.dot(lo, weights, preferred_element_type=F32))


def _headnorm_rope_kernel(x_ref, g_ref, *rest, nheads, shifts):
    table_refs, o_ref = rest[:-1], rest[-1]
    tables = [t[...] for t in table_refs]
    for h in range(nheads):
        cols = slice(h * HEAD_DIM, (h + 1) * HEAD_DIM)
        x = x_ref[:, cols].astype(F32)
        ms = _row_meansq(x)
        y = x * lax.rsqrt(ms + EPS) * g_ref[:, cols]
        o_ref[:, cols] = _rope(y, tables, shifts).astype(o_ref.dtype)


def _headnorm_rope(p, col_off, width, gains, rope, seq, tm):
    m = p.shape[0]
    assert col_off % width == 0
    tables, shifts = rope
    cb = col_off // width
    sb = seq // tm
    kern = functools.partial(_headnorm_rope_kernel, nheads=width // HEAD_DIM, shifts=shifts)
    tab_spec = pl.BlockSpec((tm, LANES), lambda i: (i % sb, 0))
    return pl.pallas_call(
        kern,
        grid=(m // tm,),
        in_specs=[
            pl.BlockSpec((tm, width), lambda i: (i, cb)),
            pl.BlockSpec((1, width), lambda i: (0, 0)),
        ] + [tab_spec] * len(tables),
        out_specs=pl.BlockSpec((tm, width), lambda i: (i, 0)),
        out_shape=jax.ShapeDtypeStruct((m, width), BF16),
        compiler_params=_params(
            ("parallel",),
            2 * _nbytes((tm, width), BF16), len(tables) * _nbytes((tm, LANES), F32),
            scratch_bytes=4 * _nbytes((tm, LANES), F32)),
        name="headnorm_rope",
    )(p, gains, *tables)


def _mla_prep_kernel(cq_ref, ckv_ref, kpe_ref, wq_ref, wk_ref, wv_ref, gql_ref, gkl_ref,
                     gqh_ref, gkn_ref, gkr_ref, *rest, shifts):
    tables = [t[...] for t in rest[:-3]]
    q_ref, k_ref, v_ref = rest[-3:]
    sumsq = lambda t: jnp.sum(t * t, axis=-1, keepdims=True)

    cq = cq_ref[...].astype(F32)
    cqn = cq * lax.rsqrt(sumsq(cq) * (1.0 / MLA_Q_RANK) + EPS) * gql_ref[...]
    q = jnp.dot(cqn.astype(BF16), wq_ref[...], preferred_element_type=F32)
    gqh = gqh_ref[...]
    for h in range(MLA_HEADS):
        nope = slice(h * MLA_QK_PAD, h * MLA_QK_PAD + LANES)
        rope = slice(h * MLA_QK_PAD + LANES, (h + 1) * MLA_QK_PAD)
        inv = lax.rsqrt(sumsq(q[:, h * MLA_QK_PAD:(h + 1) * MLA_QK_PAD]) * (1.0 / MLA_QK) + EPS)
        q_ref[:, nope] = (q[:, nope] * inv * gqh[:, :LANES]).astype(q_ref.dtype)
        q_ref[:, rope] = _rope(q[:, rope] * inv * gqh[:, LANES:], tables, shifts).astype(q_ref.dtype)

    ckv = ckv_ref[...].astype(F32)
    ckvn = (ckv * lax.rsqrt(sumsq(ckv) * (1.0 / MLA_KV_RANK) + EPS) * gkl_ref[...]).astype(BF16)
    kn = jnp.dot(ckvn, wk_ref[...], preferred_element_type=F32)
    v_ref[...] = jnp.dot(ckvn, wv_ref[...], preferred_element_type=F32).astype(v_ref.dtype)
    kpe = kpe_ref[...].astype(F32)
    pe_sq = sumsq(kpe)
    gkn = gkn_ref[...]
    kpe_rot = _rope(kpe * gkr_ref[...], tables, shifts)
    for h in range(MLA_HEADS):
        kh = kn[:, h * MLA_NOPE:(h + 1) * MLA_NOPE]
        inv = lax.rsqrt((sumsq(kh) + pe_sq) * (1.0 / MLA_QK) + EPS)
        k_ref[:, h * MLA_QK_PAD:h * MLA_QK_PAD + LANES] = (kh * inv * gkn).astype(k_ref.dtype)
        k_ref[:, h * MLA_QK_PAD + LANES:(h + 1) * MLA_QK_PAD] = (kpe_rot * inv).astype(k_ref.dtype)


def _mla_prep(p, off_cq, off_ckv, off_kpe, wq, wk, wv, gql, gkl, gqh, gkn, gkr, rope, seq, tm):
    m = p.shape[0]
    tables, shifts = rope
    sb = seq // tm
    const = lambda shape: pl.BlockSpec(shape, lambda i: (0, 0))
    tab_spec = pl.BlockSpec((tm, LANES), lambda i: (i % sb, 0))
    qk_cols = MLA_HEADS * MLA_QK_PAD
    v_cols = MLA_HEADS * MLA_V
    return pl.pallas_call(
        functools.partial(_mla_prep_kernel, shifts=shifts),
        grid=(m // tm,),
        in_specs=[
            pl.BlockSpec((tm, MLA_Q_RANK), lambda i: (i, off_cq // MLA_Q_RANK)),
            pl.BlockSpec((tm, MLA_KV_RANK), lambda i: (i, off_ckv // MLA_KV_RANK)),
            pl.BlockSpec((tm, LANES), lambda i: (i, off_kpe // LANES)),
            const(wq.shape), const(wk.shape), const(wv.shape),
            const(gql.shape), const(gkl.shape), const(gqh.shape), const(gkn.shape), const(gkr.shape),
        ] + [tab_spec] * len(tables),
        out_specs=[
            pl.BlockSpec((tm, qk_cols), lambda i: (i, 0)),
            pl.BlockSpec((tm, qk_cols), lambda i: (i, 0)),
            pl.BlockSpec((tm, v_cols), lambda i: (i, 0)),
        ],
        out_shape=[
            jax.ShapeDtypeStruct((m, qk_cols), BF16),
            jax.ShapeDtypeStruct((m, qk_cols), BF16),
            jax.ShapeDtypeStruct((m, v_cols), BF16),
        ],
        compiler_params=_params(
            ("parallel",),
            _nbytes((tm, MLA_Q_RANK + MLA_KV_RANK + LANES), BF16),
            _nbytes(wq.shape, BF16), _nbytes(wk.shape, BF16), _nbytes(wv.shape, BF16),
            _nbytes((tm, 2 * qk_cols + v_cols), BF16), len(tables) * _nbytes((tm, LANES), F32),
            scratch_bytes=3 * _nbytes((tm, qk_cols), F32)),
        name="mla_prep",
    )(p, p, p, wq, wk, wv, gql, gkl, gqh, gkn, gkr, *tables)


def _attn_kernel(q_ref, k_ref, v_ref, o_ref, *, sub, heads, dk, dv, shared_kv):
    for h in range(heads):
        kv = 0 if shared_kv else h
        k = k_ref[:, kv * dk:(kv + 1) * dk]
        v = v_ref[:, kv * dv:(kv + 1) * dv]
        v_ones = jnp.concatenate([v, jnp.ones_like(v)], axis=1)
        for r0 in range(0, q_ref.shape[0], sub):
            rows = slice(r0, r0 + sub)
            s = lax.dot_general(q_ref[rows, h * dk:(h + 1) * dk], k, (((1,), (1,)), ((), ())),
                                preferred_element_type=F32)
            p = jnp.exp2(s - jnp.max(s, axis=-1, keepdims=True)).astype(BF16)
            ov = jnp.dot(p, v_ones, preferred_element_type=F32)
            o_ref[rows, h * dv:(h + 1) * dv] = (ov[:, :dv] * (1.0 / ov[:, dv:])).astype(o_ref.dtype)


def _attention(q_arr, k_arr, v_arr, q_off, k_off, v_off, nheads, group, dk, dv, batch, seq):
    hps = ATTN_HEADS_PER_STEP
    shared_kv = group % hps == 0
    assert nheads % hps == 0 and (shared_kv or group == 1)
    kv_w = 1 if shared_kv else hps
    assert q_off % (hps * dk) == 0 and k_off % (kv_w * dk) == 0 and v_off % (kv_w * dv) == 0
    qb, kb, vb = q_off // (hps * dk), k_off // (kv_w * dk), v_off // (kv_w * dv)
    kv_of = (lambda j: (j * hps) // group) if shared_kv else (lambda j: j)
    sub = min(seq, ATTN_SUB_ROWS)
    kern = functools.partial(_attn_kernel, sub=sub, heads=hps, dk=dk, dv=dv, shared_kv=shared_kv)
    return pl.pallas_call(
        kern,
        grid=(batch, nheads // hps),
        in_specs=[
            pl.BlockSpec((seq, hps * dk), lambda b, j: (b, qb + j)),
            pl.BlockSpec((seq, kv_w * dk), lambda b, j: (b, kb + kv_of(j))),
            pl.BlockSpec((seq, kv_w * dv), lambda b, j: (b, vb + kv_of(j))),
        ],
        out_specs=pl.BlockSpec((seq, hps * dv), lambda b, j: (b, j)),
        out_shape=jax.ShapeDtypeStruct((batch * seq, nheads * dv), BF16),
        compiler_params=_params(
            ("parallel", "parallel"),
            _nbytes((seq, hps * dk), BF16), _nbytes((seq, kv_w * dk), BF16),
            _nbytes((seq, kv_w * dv), BF16), _nbytes((seq, hps * dv), BF16),
            scratch_bytes=4 * _nbytes((sub, seq), F32)),
        name="attention",
    )(q_arr, k_arr, v_arr)


def _dilated_kernel(q0_ref, q1_ref, q2_ref, k0_ref, k1_ref, k2_ref, v0_ref, v1_ref, v2_ref,
                    gq_ref, gk_ref, *rest, seq, patterns, shifts):
    n_tab = len(shifts) + 1
    table_refs, out_ref = rest[:n_tab], rest[n_tab]
    qb_ref, kb_ref, qf_ref, kf_ref, vf_ref, of_ref, lf_ref, s_ref, p_ref, m_ref = rest[n_tab + 1:]
    q_refs, k_refs, v_refs = (q0_ref, q1_ref, q2_ref), (k0_ref, k1_ref, k2_ref), (v0_ref, v1_ref, v2_ref)
    strided = [g for g, (_, dil) in enumerate(patterns) if dil > 1]
    dense = [g for g, (_, dil) in enumerate(patterns) if dil == 1]

    def prepared(src, gain, rows):
        x = src[rows, :].astype(F32)
        ms = _row_meansq(x)
        return _rope(x * lax.rsqrt(ms + EPS) * gain[...], [t[rows, :] for t in table_refs], shifts)

    for g, (window, dil) in enumerate(patterns):
        radius = window // (2 * dil)
        length = seq // dil
        kw = min(length, QBLOCK + 2 * radius)
        slot = strided.index(g) if dil > 1 else dense.index(g)
        for r0 in range(0, seq, DIL_PREP_ROWS):
            chunk = slice(r0, r0 + DIL_PREP_ROWS)
            if dil > 1:
                qf_ref[slot, chunk, :] = prepared(q_refs[g], gq_ref, chunk)
                kf_ref[slot, chunk, :] = prepared(k_refs[g], gk_ref, chunk)
                vf_ref[slot, chunk, :] = v_refs[g][chunk, :].astype(F32)
            else:
                qb_ref[slot, chunk, :] = prepared(q_refs[g], gq_ref, chunk).astype(BF16)
                kb_ref[slot, chunk, :] = prepared(k_refs[g], gk_ref, chunk).astype(BF16)
        def rows(r, start, n, dil=dil):
            return pl.ds(start, n) if dil == 1 else pl.ds(start * dil + r, n, stride=dil)

        def window_of(dense_ref, copy, r, start, n, dil=dil, slot=slot, rows=rows):
            if dil == 1:
                return dense_ref[rows(r, start, n), :]
            return copy[slot, rows(r, start, n), :].astype(BF16)

        tiles = [(r, t * QBLOCK, min(max(t * QBLOCK - radius, 0), length - kw))
                 for r in range(dil) for t in range(length // QBLOCK)]
        for i, (r, q0, ks) in enumerate(tiles):
            q = window_of(qb_ref.at[slot] if dil == 1 else None, qf_ref, r, q0, QBLOCK)
            k = window_of(kb_ref.at[slot] if dil == 1 else None, kf_ref, r, ks, kw)
            s = lax.dot_general(q, k, (((1,), (1,)), ((), ())), preferred_element_type=F32)
            rel = (lax.broadcasted_iota(jnp.int32, s.shape, 0)
                   - lax.broadcasted_iota(jnp.int32, s.shape, 1)) + (q0 - ks)
            s_ref[i, :, :kw] = jnp.where(jnp.abs(rel) <= radius, s, NEG_INF)
        for i in range(len(tiles)):
            s = s_ref[i, :, :kw]
            m = jnp.max(s, axis=-1, keepdims=True)
            p_ref[i, :, :kw] = jnp.exp(s - m).astype(BF16)
            m_ref[i] = jnp.broadcast_to(m, (QBLOCK, HEAD_DIM))
        for i, (r, q0, ks) in enumerate(tiles):
            v = window_of(v_refs[g], vf_ref, r, ks, kw)
            ov = jnp.dot(p_ref[i, :, :kw], jnp.concatenate([v, jnp.ones_like(v)], axis=1),
                         preferred_element_type=F32)
            l = ov[:, HEAD_DIM:]
            of_ref[g, rows(r, q0, QBLOCK), :] = ov[:, :HEAD_DIM] * (1.0 / l)
            lf_ref[g, rows(r, q0, QBLOCK), :] = m_ref[i] + jnp.log(l)

    la, lb, lc = lf_ref[0], lf_ref[1], lf_ref[2]
    m = jnp.maximum(jnp.maximum(la, lb), lc)
    ea, eb, ec = jnp.exp(la - m), jnp.exp(lb - m), jnp.exp(lc - m)
    num = ea * of_ref[0] + eb * of_ref[1] + ec * of_ref[2]
    out_ref[...] = (num * (1.0 / (ea + eb + ec))).astype(out_ref.dtype)


def _dilated(p, off_q, off_k, off_v, gq, gk, rope, batch, seq):
    ng = len(DIL_PATTERNS)
    n_strided = sum(dil > 1 for _, dil in DIL_PATTERNS)
    tables, shifts = rope
    kw_max = max(min(seq // dil, QBLOCK + window // dil) for window, dil in DIL_PATTERNS)
    kern = functools.partial(_dilated_kernel, seq=seq, patterns=DIL_PATTERNS, shifts=shifts)
    group_w = DIL_HPG * HEAD_DIM

    def head(off, g):
        assert off % HEAD_DIM == 0
        cb = (off + g * group_w) // HEAD_DIM
        return pl.BlockSpec((seq, HEAD_DIM), lambda b, h: (b, cb + h))

    const = lambda shape: pl.BlockSpec(shape, lambda b, h: (0, 0))
    slab = _nbytes((seq, HEAD_DIM), F32)
    return pl.pallas_call(
        kern,
        grid=(batch, DIL_HPG),
        in_specs=([head(off_q, g) for g in range(ng)] + [head(off_k, g) for g in range(ng)]
                  + [head(off_v, g) for g in range(ng)]
                  + [const((1, HEAD_DIM))] * 2 + [const((seq, LANES))] * len(tables)),
        out_specs=pl.BlockSpec((seq, HEAD_DIM), lambda b, h: (b, h)),
        out_shape=jax.ShapeDtypeStruct((batch * seq, group_w), BF16),
        scratch_shapes=[pltpu.VMEM((ng - n_strided, seq, HEAD_DIM), BF16)] * 2
                       + [pltpu.VMEM((n_strided, seq, HEAD_DIM), F32)] * 3
                       + [pltpu.VMEM((ng, seq, HEAD_DIM), F32)] * 2
                       + [pltpu.VMEM((seq // QBLOCK, QBLOCK, kw_max), F32),
                          pltpu.VMEM((seq // QBLOCK, QBLOCK, kw_max), BF16),
                          pltpu.VMEM((seq // QBLOCK, QBLOCK, HEAD_DIM), F32)],
        compiler_params=_params(
            ("parallel", "parallel"),
            (3 * ng + 1) * _nbytes((seq, HEAD_DIM), BF16), len(tables) * slab,
            scratch_bytes=(ng - n_strided + 3 * n_strided + 2 * ng) * slab + 4 * slab),
        name="dilated",
    )(*([p] * (3 * ng)), gq, gk, *tables)


def _merge_kernel(xn_ref, oa_ref, ob_ref, oc_ref, wga_ref, wgb_ref, wgc_ref, bga_ref, bgb_ref, bgc_ref,
                  wa_ref, wb_ref, wc_ref, o_ref):
    xn = xn_ref[...]
    merged = None
    for o_i, w_i, wg_i, bg_i in ((oa_ref, wa_ref, wga_ref, bga_ref), (ob_ref, wb_ref, wgb_ref, bgb_ref),
                                 (oc_ref, wc_ref, wgc_ref, bgc_ref)):
        logit = lax.dot_general(xn, wg_i[...], NT_DIMS, preferred_element_type=F32) + bg_i[...]
        gate = 1.0 / (1.0 + jnp.exp(-logit))
        term = gate * jnp.dot(o_i[...], w_i[...], preferred_element_type=F32)
        merged = term if merged is None else merged + term
    o_ref[...] = merged.astype(o_ref.dtype)


def _merge(xn, oa, ob, oc, w_gate, b_gate, w_oa, w_ob, w_oc, layer, tm, tn):
    m, d = xn.shape
    nj = d // tn
    row = lambda a: pl.BlockSpec((tm, a.shape[1]), lambda i, j: (i, 0))
    wcol = lambda w: pl.BlockSpec((None, w.shape[1], tn), lambda i, j: (layer, 0, j))
    wgate = lambda g: pl.BlockSpec((tn, d), lambda i, j: (g * nj + j, 0))
    bgate = lambda g: pl.BlockSpec((1, tn), lambda i, j: (0, g * nj + j))
    k_branches = oa.shape[1] + ob.shape[1] + oc.shape[1]
    return pl.pallas_call(
        _merge_kernel,
        grid=(m // tm, nj),
        in_specs=[row(xn), row(oa), row(ob), row(oc), wgate(0), wgate(1), wgate(2),
                  bgate(0), bgate(1), bgate(2), wcol(w_oa), wcol(w_ob), wcol(w_oc)],
        out_specs=pl.BlockSpec((tm, tn), lambda i, j: (i, j)),
        out_shape=jax.ShapeDtypeStruct((m, d), BF16),
        compiler_params=_params(
            ("parallel", "parallel"),
            _nbytes((tm, d + k_branches), BF16), _nbytes((tm, tn), BF16),
            _nbytes((3 * d + k_branches, tn), BF16),
            scratch_bytes=6 * _nbytes((tm, tn), F32)),
        name="merge",
    )(xn, oa, ob, oc, w_gate, w_gate, w_gate, b_gate, b_gate, b_gate, w_oa, w_ob, w_oc)


def _resid_matmul_kernel(x_ref, a_ref, w_ref, o_ref):
    o_ref[...] = x_ref[...] + jnp.dot(a_ref[...], w_ref[...], preferred_element_type=F32)


def _resid_matmul(x2d, a, w, layer, tm, tn):
    m, d = x2d.shape
    k = a.shape[1]
    return pl.pallas_call(
        _resid_matmul_kernel,
        grid=(m // tm, d // tn),
        in_specs=[
            pl.BlockSpec((tm, tn), lambda i, j: (i, j)),
            pl.BlockSpec((tm, k), lambda i, j: (i, 0)),
            pl.BlockSpec((None, k, tn), lambda i, j: (layer, 0, j)),
        ],
        out_specs=pl.BlockSpec((tm, tn), lambda i, j: (i, j)),
        out_shape=jax.ShapeDtypeStruct((m, d), F32),
        compiler_params=_params(
            ("parallel", "parallel"),
            2 * _nbytes((tm, tn), F32), _nbytes((tm, k), BF16), _nbytes((k, tn), BF16)),
        name="out_proj",
    )(x2d, a, w)


def _mlp_kernel(x_ref, g_ref, wu_ref, wd_ref, o_ref, xn_ref):
    c = pl.program_id(1)

    @pl.when(c == 0)
    def _():
        x = x_ref[...]
        ms = jnp.mean(x * x, axis=-1, keepdims=True)
        xn_ref[...] = (x * lax.rsqrt(ms + EPS) * g_ref[...]).astype(BF16)
        o_ref[...] = x

    h = jnp.dot(xn_ref[...], wu_ref[...].astype(BF16), preferred_element_type=F32)
    h = jnp.square(jnp.maximum(h, 0.0)).astype(BF16)
    o_ref[...] += jnp.dot(h, wd_ref[...].astype(BF16), preferred_element_type=F32)


def _mlp(x2d, gain, w_up, w_down, layer, tm, tf):
    m, d = x2d.shape
    f = w_up.shape[2]
    return pl.pallas_call(
        _mlp_kernel,
        grid=(m // tm, f // tf),
        in_specs=[
            pl.BlockSpec((tm, d), lambda i, c: (i, 0)),
            pl.BlockSpec((1, d), lambda i, c: (0, 0)),
            pl.BlockSpec((None, d, tf), lambda i, c: (layer, 0, c)),
            pl.BlockSpec((None, tf, d), lambda i, c: (layer, c, 0)),
        ],
        out_specs=pl.BlockSpec((tm, d), lambda i, c: (i, 0)),
        out_shape=jax.ShapeDtypeStruct((m, d), F32),
        scratch_shapes=[pltpu.VMEM((tm, d), BF16)],
        compiler_params=_params(
            ("parallel", "arbitrary"),
            2 * _nbytes((tm, d), F32), 2 * _nbytes((d, tf), F32),
            scratch_bytes=_nbytes((tm, d), BF16) + _nbytes((tm, tf), F32)),
        name="mlp",
    )(x2d, gain, w_up, w_down)


def _largest_tile(n, cap):
    t = min(n, cap)
    while n % t:
        t //= 2
    return t


def kernel(x, attn_norm, w_in, b_gate, mla_q_lat_norm, w_uq, mla_kv_lat_norm, w_ukv,
           mla_q_head_norm, mla_k_head_norm, gqa_q_norm, gqa_k_norm, dil_q_norm, dil_k_norm,
           w_oa, w_ob, w_oc, w_out, mlp_norm, w_up, w_down):
    batch, seq, d = x.shape
    depth = w_in.shape[0]
    m = batch * seq
    d_ff = w_up.shape[2]
    gate_cols = 3 * d
    assert w_in.shape[2] == A_COLS + B_COLS + C_COLS + gate_cols
    assert seq % GRID_W == 0 and seq % (16 * QBLOCK) == 0 and d % LANES == 0

    tn_proj = 1792
    nq_b, nk_b = GQA_HEADS * HEAD_DIM, GQA_KV_HEADS * HEAD_DIM
    nh_c = DIL_HEADS * HEAD_DIM
    off_bq = 0
    off_bk = off_bq + nq_b
    off_bv = off_bk + nk_b
    off_cq = off_bv + nk_b
    off_ck = off_cq + nh_c
    off_cv = off_ck + nh_c
    off_aq = off_cv + nh_c
    off_akv = off_aq + MLA_Q_RANK
    off_ape = off_akv + MLA_KV_RANK
    n_proj = _round_up(off_ape + LANES, tn_proj)

    tm_big = _largest_tile(m, 1024)
    tm_half = _largest_tile(m, 512)
    tm_seq = _largest_tile(seq, 1024)
    tm_mla = _largest_tile(seq, 512)

    pos = np.arange(seq, dtype=np.float64)
    row = np.repeat(np.arange(seq // GRID_W, dtype=np.float64), GRID_W)
    col = np.tile(np.arange(GRID_W, dtype=np.float64), seq // GRID_W)
    half_mla, half_ax, half_dil = MLA_ROPE // 2, HEAD_DIM // 4, PARTIAL_ROPE_DIM // 2
    rope_mla = _rope_tables(seq, half_mla, [(0, half_mla, pos, MLA_ROPE_THETA)])
    rope_gqa = _rope_tables(seq, half_ax, [(0, half_ax, row, AXIAL_THETA),
                                           (2 * half_ax, half_ax, col, AXIAL_THETA)])
    rope_dil = _rope_tables(seq, half_dil, [(0, half_dil, pos, PARTIAL_ROPE_THETA)])

    w_oa_b, w_ob_b, w_oc_b, w_out_b = (w.astype(BF16) for w in (w_oa, w_ob, w_oc, w_out))
    qk_pad = MLA_QK_PAD - MLA_QK
    w_in_t = jnp.swapaxes(w_in, 1, 2)
    c1 = A_COLS + B_COLS + C_COLS
    tr_w = math.gcd(c1 - A_COLS, gate_cols, 1024)
    a_blocks = -(-A_COLS // tr_w)
    starts_p = [A_COLS + r * tr_w for r in range((c1 - A_COLS) // tr_w)] + [k * tr_w for k in range(a_blocks)]
    valid_p = ([tr_w] * ((c1 - A_COLS) // tr_w)
               + [min(tr_w, A_COLS - k * tr_w) for k in range(a_blocks)])
    starts_g = [c1 + r * tr_w for r in range(gate_cols // tr_w)]
    assert len(starts_p) * tr_w == n_proj

    x2d = x.reshape(m, d)
    for l in range(depth):
        w_p = _gather_rows(w_in_t, l, starts_p, valid_p, tr_w)
        w_g = _gather_rows(w_in_t, l, starts_g, [tr_w] * len(starts_g), tr_w)
        p, xn = _proj(x2d, attn_norm[l][None, :], w_p, tm_big, tn_proj)

        wq = jnp.pad(w_uq[l].reshape(MLA_Q_RANK, MLA_HEADS, MLA_QK), ((0, 0), (0, 0), (0, qk_pad))
                     ).reshape(MLA_Q_RANK, MLA_HEADS * MLA_QK_PAD).astype(BF16)
        wkv = w_ukv[l].reshape(MLA_KV_RANK, MLA_HEADS, MLA_NOPE + MLA_V)
        wk = wkv[:, :, :MLA_NOPE].reshape(MLA_KV_RANK, MLA_HEADS * MLA_NOPE).astype(BF16)
        wv = wkv[:, :, MLA_NOPE:].reshape(MLA_KV_RANK, MLA_HEADS * MLA_V).astype(BF16)
        gqh = jnp.pad(mla_q_head_norm[l] * (LOG2_E * MLA_QK ** -0.5), (0, qk_pad))[None, :]
        gkn = mla_k_head_norm[l][None, :MLA_NOPE]
        gkr = jnp.pad(mla_k_head_norm[l][MLA_NOPE:], (0, LANES - MLA_ROPE))[None, :]
        qa, ka, va = _mla_prep(p, off_aq, off_akv, off_ape, wq, wk, wv,
                               mla_q_lat_norm[l][None, :], mla_kv_lat_norm[l][None, :],
                               gqh, gkn, gkr, rope_mla, seq, tm_mla)
        oa = _attention(qa, ka, va, 0, 0, 0, MLA_HEADS, 1, MLA_QK_PAD, MLA_V, batch, seq)

        scale = HEAD_DIM ** -0.5
        gq = jnp.tile(gqa_q_norm[l] * (LOG2_E * scale), GQA_HEADS)[None, :]
        gk = jnp.tile(gqa_k_norm[l], GQA_KV_HEADS)[None, :]
        qb = _headnorm_rope(p, off_bq, nq_b, gq, rope_gqa, seq, tm_seq)
        kb = _headnorm_rope(p, off_bk, nk_b, gk, rope_gqa, seq, tm_seq)
        ob = _attention(qb, kb, p, 0, 0, off_bv, GQA_HEADS, GQA_GROUP, HEAD_DIM, HEAD_DIM,
                        batch, seq)

        oc = _dilated(p, off_cq, off_ck, off_cv, (dil_q_norm[l] * scale)[None, :],
                      dil_k_norm[l][None, :], rope_dil, batch, seq)

        merged = _merge(xn, oa, ob, oc, w_g, b_gate[l][None, :], w_oa_b, w_ob_b, w_oc_b, l,
                        tm_big, _largest_tile(d, 512))
        x2d = _resid_matmul(x2d, merged, w_out_b, l, tm_half, d)
        x2d = _mlp(x2d, mlp_norm[l][None, :], w_up, w_down, l,
                   _largest_tile(m, 1024), _largest_tile(d_ff, 512))
    return x2d.reshape(batch, seq, d)
```

```python
import functools
import math

import numpy as np
import jax
import jax.numpy as jnp
from jax import lax
from jax.experimental import pallas as pl
from jax.experimental.pallas import tpu as pltpu

F32 = jnp.float32
BF16 = jnp.bfloat16

LANES = 128
VMEM_CAP_BYTES = 60000 * 1024

HEAD_DIM = 128
GRID_W = 64
EPS = 1e-6
NEG_INF = -1e30
NT_DIMS = (((1,), (1,)), ((), ()))

MLA_HEADS = 8
MLA_Q_RANK = 512
MLA_KV_RANK = 256
MLA_NOPE = 128
MLA_ROPE = 64
MLA_V = 128
MLA_ROPE_THETA = 10000.0
MLA_QK = MLA_NOPE + MLA_ROPE
MLA_QK_PAD = 2 * LANES

GQA_HEADS = 8
GQA_KV_HEADS = 2
GQA_GROUP = GQA_HEADS // GQA_KV_HEADS
AXIAL_THETA = 10000.0

DIL_PATTERNS = ((128, 1), (512, 4), (2048, 16))
DIL_HPG = 4
DIL_HEADS = DIL_HPG * len(DIL_PATTERNS)
PARTIAL_ROPE_DIM = HEAD_DIM // 4
PARTIAL_ROPE_THETA = 500000.0
QBLOCK = 128
ATTN_SUB_ROWS = 256
LOG2_E = 1.4426950408889634
ATTN_HEADS_PER_STEP = 2
DIL_PREP_ROWS = 256

A_COLS = MLA_Q_RANK + MLA_KV_RANK + MLA_ROPE
B_COLS = (GQA_HEADS + 2 * GQA_KV_HEADS) * HEAD_DIM
C_COLS = 3 * DIL_HEADS * HEAD_DIM


def _round_up(n, m):
    return -(-n // m) * m


def _params(semantics, *block_bytes, scratch_bytes=0):
    need = 2 * sum(block_bytes) + scratch_bytes
    limit = min(VMEM_CAP_BYTES, need + (24 << 20))
    return pltpu.CompilerParams(dimension_semantics=semantics, vmem_limit_bytes=limit)


def _nbytes(shape, dtype):
    n = 1
    for s in shape:
        n *= s
    return n * jnp.dtype(dtype).itemsize


def _rope_tables(seq, dist, segments):
    cos = np.ones((seq, LANES), np.float64)
    sin_x1 = np.zeros((seq, LANES), np.float64)
    sin_x2 = np.zeros((seq, LANES), np.float64)
    for first, half, pos, theta in segments:
        inv = theta ** (-np.arange(half, dtype=np.float64) / half)
        ang = pos[:, None] * inv[None, :]
        c, s = np.cos(ang), np.sin(ang)
        hi = first + dist
        cos[:, first:first + half] = c
        cos[:, hi:hi + half] = c
        sin_x1[:, first:first + half] = -s
        sin_x2[:, hi:hi + half] = s
    as_table = lambda t: jnp.asarray(t.astype(np.float32))
    if 2 * dist == LANES:
        return (as_table(cos), as_table(sin_x1 + sin_x2)), (dist,)
    return (as_table(cos), as_table(sin_x1), as_table(sin_x2)), (LANES - dist, dist)


def _rope(x, tables, shifts):
    out = x * tables[0]
    for table, shift in zip(tables[1:], shifts):
        out = out + pltpu.roll(x, shift, 1) * table
    return out


def _proj_kernel(x_ref, g_ref, w_ref, o_ref, xn_ref):
    @pl.when(pl.program_id(1) == 0)
    def _():
        x = x_ref[...]
        ms = jnp.mean(x * x, axis=-1, keepdims=True)
        xn_ref[...] = (x * lax.rsqrt(ms + EPS) * g_ref[...]).astype(BF16)

    o_ref[...] = lax.dot_general(xn_ref[...], w_ref[...], NT_DIMS,
                                 preferred_element_type=F32).astype(o_ref.dtype)


def _proj(x2d, gain, w_t, tm, tn):
    m, d = x2d.shape
    n = w_t.shape[0]
    return pl.pallas_call(
        _proj_kernel,
        grid=(m // tm, n // tn),
        in_specs=[
            pl.BlockSpec((tm, d), lambda i, j: (i, 0)),
            pl.BlockSpec((1, d), lambda i, j: (0, 0)),
            pl.BlockSpec((tn, d), lambda i, j: (j, 0)),
        ],
        out_specs=[pl.BlockSpec((tm, tn), lambda i, j: (i, j)),
                   pl.BlockSpec((tm, d), lambda i, j: (i, 0))],
        out_shape=[jax.ShapeDtypeStruct((m, n), BF16), jax.ShapeDtypeStruct((m, d), BF16)],
        compiler_params=_params(
            ("parallel", "arbitrary"),
            _nbytes((tm, d), F32), _nbytes((d, tn), BF16), _nbytes((tm, tn), BF16),
            _nbytes((tm, d), BF16), scratch_bytes=_nbytes((tm, d), F32)),
        name="proj",
    )(x2d, gain, w_t)


def _gather_rows_kernel(w_ref, o_ref, *, valid_rows):
    r = pl.program_id(0)
    limit = functools.reduce(lambda acc, t: jnp.where(r == t[0], t[1], acc),
                             enumerate(valid_rows), o_ref.shape[0])
    row = lax.broadcasted_iota(jnp.int32, o_ref.shape, 0)
    o_ref[...] = jnp.where(row < limit, w_ref[...], 0.0).astype(o_ref.dtype)


def _gather_rows(w_t, layer, starts, valid_rows, tr):
    _, n, d = w_t.shape
    assert all(s % 8 == 0 and s + tr <= n for s in starts)
    kern = functools.partial(_gather_rows_kernel, valid_rows=tuple(valid_rows))

    def start_of(r):
        start = functools.reduce(lambda acc, t: jnp.where(r == t[0], t[1], acc), enumerate(starts), 0)
        return pl.multiple_of(start, 8)

    return pl.pallas_call(
        kern,
        grid=(len(starts),),
        in_specs=[pl.BlockSpec((None, pl.Element(tr), pl.Element(d)),
                               lambda r: (layer, start_of(r), 0))],
        out_specs=pl.BlockSpec((tr, d), lambda r: (r, 0)),
        out_shape=jax.ShapeDtypeStruct((len(starts) * tr, d), BF16),
        compiler_params=_params(("parallel",), _nbytes((tr, d), F32), _nbytes((tr, d), BF16)),
        name="gather_rows",
    )(w_t)


def _row_meansq(x):
    n = x.shape[-1]
    assert n & (n - 1) == 0
    sq = x * x
    hi = sq.astype(BF16)
    lo = (sq - hi.astype(F32)).astype(BF16)
    weights = jnp.full((n, LANES), 1.0 / n, BF16)
    return (jnp.dot(hi, weights, preferred_element_type=F32)
            + jnp.dot(lo, weights, preferred_element_type=F32))


def _headnorm_rope_kernel(x_ref, g_ref, *rest, nheads, shifts):
    table_refs, o_ref = rest[:-1], rest[-1]
    tables = [t[...] for t in table_refs]
    for h in range(nheads):
        cols = slice(h * HEAD_DIM, (h + 1) * HEAD_DIM)
        x = x_ref[:, cols].astype(F32)
        ms = _row_meansq(x)
        y = x * lax.rsqrt(ms + EPS) * g_ref[:, cols]
        o_ref[:, cols] = _rope(y, tables, shifts).astype(o_ref.dtype)


def _headnorm_rope(p, col_off, width, gains, rope, seq, tm):
    m = p.shape[0]
    assert col_off % width == 0
    tables, shifts = rope
    cb = col_off // width
    sb = seq // tm
    kern = functools.partial(_headnorm_rope_kernel, nheads=width // HEAD_DIM, shifts=shifts)
    tab_spec = pl.BlockSpec((tm, LANES), lambda i: (i % sb, 0))
    return pl.pallas_call(
        kern,
        grid=(m // tm,),
        in_specs=[
            pl.BlockSpec((tm, width), lambda i: (i, cb)),
            pl.BlockSpec((1, width), lambda i: (0, 0)),
        ] + [tab_spec] * len(tables),
        out_specs=pl.BlockSpec((tm, width), lambda i: (i, 0)),
        out_shape=jax.ShapeDtypeStruct((m, width), BF16),
        compiler_params=_params(
            ("parallel",),
            2 * _nbytes((tm, width), BF16), len(tables) * _nbytes((tm, LANES), F32),
            scratch_bytes=4 * _nbytes((tm, LANES), F32)),
        name="headnorm_rope",
    )(p, gains, *tables)


def _mla_prep_kernel(cq_ref, ckv_ref, kpe_ref, wq_ref, wk_ref, wv_ref, gql_ref, gkl_ref,
                     gqh_ref, gkn_ref, gkr_ref, *rest, shifts):
    tables = [t[...] for t in rest[:-3]]
    q_ref, k_ref, v_ref = rest[-3:]
    sumsq = lambda t: jnp.sum(t * t, axis=-1, keepdims=True)

    cq = cq_ref[...].astype(F32)
    cqn = cq * lax.rsqrt(sumsq(cq) * (1.0 / MLA_Q_RANK) + EPS) * gql_ref[...]
    q = jnp.dot(cqn.astype(BF16), wq_ref[...], preferred_element_type=F32)
    gqh = gqh_ref[...]
    for h in range(MLA_HEADS):
        nope = slice(h * MLA_QK_PAD, h * MLA_QK_PAD + LANES)
        rope = slice(h * MLA_QK_PAD + LANES, (h + 1) * MLA_QK_PAD)
        inv = lax.rsqrt(sumsq(q[:, h * MLA_QK_PAD:(h + 1) * MLA_QK_PAD]) * (1.0 / MLA_QK) + EPS)
        q_ref[:, nope] = (q[:, nope] * inv * gqh[:, :LANES]).astype(q_ref.dtype)
        q_ref[:, rope] = _rope(q[:, rope] * inv * gqh[:, LANES:], tables, shifts).astype(q_ref.dtype)

    ckv = ckv_ref[...].astype(F32)
    ckvn = (ckv * lax.rsqrt(sumsq(ckv) * (1.0 / MLA_KV_RANK) + EPS) * gkl_ref[...]).astype(BF16)
    kn = jnp.dot(ckvn, wk_ref[...], preferred_element_type=F32)
    v_ref[...] = jnp.dot(ckvn, wv_ref[...], preferred_element_type=F32).astype(v_ref.dtype)
    kpe = kpe_ref[...].astype(F32)
    pe_sq = sumsq(kpe)
    gkn = gkn_ref[...]
    kpe_rot = _rope(kpe * gkr_ref[...], tables, shifts)
    for h in range(MLA_HEADS):
        kh = kn[:, h * MLA_NOPE:(h + 1) * MLA_NOPE]
        inv = lax.rsqrt((sumsq(kh) + pe_sq) * (1.0 / MLA_QK) + EPS)
        k_ref[:, h * MLA_QK_PAD:h * MLA_QK_PAD + LANES] = (kh * inv * gkn).astype(k_ref.dtype)
        k_ref[:, h * MLA_QK_PAD + LANES:(h + 1) * MLA_QK_PAD] = (kpe_rot * inv).astype(k_ref.dtype)


def _mla_prep(p, off_cq, off_ckv, off_kpe, wq, wk, wv, gql, gkl, gqh, gkn, gkr, rope, seq, tm):
    m = p.shape[0]
    tables, shifts = rope
    sb = seq // tm
    const = lambda shape: pl.BlockSpec(shape, lambda i: (0, 0))
    tab_spec = pl.BlockSpec((tm, LANES), lambda i: (i % sb, 0))
    qk_cols = MLA_HEADS * MLA_QK_PAD
    v_cols = MLA_HEADS * MLA_V
    return pl.pallas_call(
        functools.partial(_mla_prep_kernel, shifts=shifts),
        grid=(m // tm,),
        in_specs=[
            pl.BlockSpec((tm, MLA_Q_RANK), lambda i: (i, off_cq // MLA_Q_RANK)),
            pl.BlockSpec((tm, MLA_KV_RANK), lambda i: (i, off_ckv // MLA_KV_RANK)),
            pl.BlockSpec((tm, LANES), lambda i: (i, off_kpe // LANES)),
            const(wq.shape), const(wk.shape), const(wv.shape),
            const(gql.shape), const(gkl.shape), const(gqh.shape), const(gkn.shape), const(gkr.shape),
        ] + [tab_spec] * len(tables),
        out_specs=[
            pl.BlockSpec((tm, qk_cols), lambda i: (i, 0)),
            pl.BlockSpec((tm, qk_cols), lambda i: (i, 0)),
            pl.BlockSpec((tm, v_cols), lambda i: (i, 0)),
        ],
        out_shape=[
            jax.ShapeDtypeStruct((m, qk_cols), BF16),
            jax.ShapeDtypeStruct((m, qk_cols), BF16),
            jax.ShapeDtypeStruct((m, v_cols), BF16),
        ],
        compiler_params=_params(
            ("parallel",),
            _nbytes((tm, MLA_Q_RANK + MLA_KV_RANK + LANES), BF16),
            _nbytes(wq.shape, BF16), _nbytes(wk.shape, BF16), _nbytes(wv.shape, BF16),
            _nbytes((tm, 2 * qk_cols + v_cols), BF16), len(tables) * _nbytes((tm, LANES), F32),
            scratch_bytes=3 * _nbytes((tm, qk_cols), F32)),
        name="mla_prep",
    )(p, p, p, wq, wk, wv, gql, gkl, gqh, gkn, gkr, *tables)


def _attn_kernel(q_ref, k_ref, v_ref, o_ref, *, sub, heads, dk, dv, shared_kv):
    for h in range(heads):
        kv = 0 if shared_kv else h
        k = k_ref[:, kv * dk:(kv + 1) * dk]
        v = v_ref[:, kv * dv:(kv + 1) * dv]
        v_ones = jnp.concatenate([v, jnp.ones_like(v)], axis=1)
        for r0 in range(0, q_ref.shape[0], sub):
            rows = slice(r0, r0 + sub)
            s = lax.dot_general(q_ref[rows, h * dk:(h + 1) * dk], k, (((1,), (1,)), ((), ())),
                                preferred_element_type=F32)
            p = jnp.exp2(s - jnp.max(s, axis=-1, keepdims=True)).astype(BF16)
            ov = jnp.dot(p, v_ones, preferred_element_type=F32)
            o_ref[rows, h * dv:(h + 1) * dv] = (ov[:, :dv] * (1.0 / ov[:, dv:])).astype(o_ref.dtype)


def _attention(q_arr, k_arr, v_arr, q_off, k_off, v_off, nheads, group, dk, dv, batch, seq):
    hps = ATTN_HEADS_PER_STEP
    shared_kv = group % hps == 0
    assert nheads % hps == 0 and (shared_kv or group == 1)
    kv_w = 1 if shared_kv else hps
    assert q_off % (hps * dk) == 0 and k_off % (kv_w * dk) == 0 and v_off % (kv_w * dv) == 0
    qb, kb, vb = q_off // (hps * dk), k_off // (kv_w * dk), v_off // (kv_w * dv)
    kv_of = (lambda j: (j * hps) // group) if shared_kv else (lambda j: j)
    sub = min(seq, ATTN_SUB_ROWS)
    kern = functools.partial(_attn_kernel, sub=sub, heads=hps, dk=dk, dv=dv, shared_kv=shared_kv)
    return pl.pallas_call(
        kern,
        grid=(batch, nheads // hps),
        in_specs=[
            pl.BlockSpec((seq, hps * dk), lambda b, j: (b, qb + j)),
            pl.BlockSpec((seq, kv_w * dk), lambda b, j: (b, kb + kv_of(j))),
            pl.BlockSpec((seq, kv_w * dv), lambda b, j: (b, vb + kv_of(j))),
        ],
        out_specs=pl.BlockSpec((seq, hps * dv), lambda b, j: (b, j)),
        out_shape=jax.ShapeDtypeStruct((batch * seq, nheads * dv), BF16),
        compiler_params=_params(
            ("parallel", "parallel"),
            _nbytes((seq, hps * dk), BF16), _nbytes((seq, kv_w * dk), BF16),
            _nbytes((seq, kv_w * dv), BF16), _nbytes((seq, hps * dv), BF16),
            scratch_bytes=4 * _nbytes((sub, seq), F32)),
        name="attention",
    )(q_arr, k_arr, v_arr)


def _dilated_kernel(q0_ref, q1_ref, q2_ref, k0_ref, k1_ref, k2_ref, v0_ref, v1_ref, v2_ref,
                    gq_ref, gk_ref, *rest, seq, patterns, shifts):
    n_tab = len(shifts) + 1
    table_refs, out_ref = rest[:n_tab], rest[n_tab]
    qb_ref, kb_ref, qf_ref, kf_ref, vf_ref, of_ref, lf_ref, s_ref, p_ref, m_ref = rest[n_tab + 1:]
    q_refs, k_refs, v_refs = (q0_ref, q1_ref, q2_ref), (k0_ref, k1_ref, k2_ref), (v0_ref, v1_ref, v2_ref)
    strided = [g for g, (_, dil) in enumerate(patterns) if dil > 1]
    dense = [g for g, (_, dil) in enumerate(patterns) if dil == 1]

    def prepared(src, gain, rows):
        x = src[rows, :].astype(F32)
        ms = _row_meansq(x)
        return _rope(x * lax.rsqrt(ms + EPS) * gain[...], [t[rows, :] for t in table_refs], shifts)

    for g, (window, dil) in enumerate(patterns):
        radius = window // (2 * dil)
        length = seq // dil
        kw = min(length, QBLOCK + 2 * radius)
        slot = strided.index(g) if dil > 1 else dense.index(g)
        for r0 in range(0, seq, DIL_PREP_ROWS):
            chunk = slice(r0, r0 + DIL_PREP_ROWS)
            if dil > 1:
                qf_ref[slot, chunk, :] = prepared(q_refs[g], gq_ref, chunk)
                kf_ref[slot, chunk, :] = prepared(k_refs[g], gk_ref, chunk)
                vf_ref[slot, chunk, :] = v_refs[g][chunk, :].astype(F32)
            else:
                qb_ref[slot, chunk, :] = prepared(q_refs[g], gq_ref, chunk).astype(BF16)
                kb_ref[slot, chunk, :] = prepared(k_refs[g], gk_ref, chunk).astype(BF16)
        def rows(r, start, n, dil=dil):
            return pl.ds(start, n) if dil == 1 else pl.ds(start * dil + r, n, stride=dil)

        def window_of(dense_ref, copy, r, start, n, dil=dil, slot=slot, rows=rows):
            if dil == 1:
                return dense_ref[rows(r, start, n), :]
            return copy[slot, rows(r, start, n), :].astype(BF16)

        tiles = [(r, t * QBLOCK, min(max(t * QBLOCK - radius, 0), length - kw))
                 for r in range(dil) for t in range(length // QBLOCK)]
        for i, (r, q0, ks) in enumerate(tiles):
            q = window_of(qb_ref.at[slot] if dil == 1 else None, qf_ref, r, q0, QBLOCK)
            k = window_of(kb_ref.at[slot] if dil == 1 else None, kf_ref, r, ks, kw)
            s = lax.dot_general(q, k, (((1,), (1,)), ((), ())), preferred_element_type=F32)
            rel = (lax.broadcasted_iota(jnp.int32, s.shape, 0)
                   - lax.broadcasted_iota(jnp.int32, s.shape, 1)) + (q0 - ks)
            s_ref[i, :, :kw] = jnp.where(jnp.abs(rel) <= radius, s, NEG_INF)
        for i in range(len(tiles)):
            s = s_ref[i, :, :kw]
            m = jnp.max(s, axis=-1, keepdims=True)
            p_ref[i, :, :kw] = jnp.exp(s - m).astype(BF16)
            m_ref[i] = jnp.broadcast_to(m, (QBLOCK, HEAD_DIM))
        for i, (r, q0, ks) in enumerate(tiles):
            v = window_of(v_refs[g], vf_ref, r, ks, kw)
            ov = jnp.dot(p_ref[i, :, :kw], jnp.concatenate([v, jnp.ones_like(v)], axis=1),
                         preferred_element_type=F32)
            l = ov[:, HEAD_DIM:]
            of_ref[g, rows(r, q0, QBLOCK), :] = ov[:, :HEAD_DIM] * (1.0 / l)
            lf_ref[g, rows(r, q0, QBLOCK), :] = m_ref[i] + jnp.log(l)

    la, lb, lc = lf_ref[0], lf_ref[1], lf_ref[2]
    m = jnp.maximum(jnp.maximum(la, lb), lc)
    ea, eb, ec = jnp.exp(la - m), jnp.exp(lb - m), jnp.exp(lc - m)
    num = ea * of_ref[0] + eb * of_ref[1] + ec * of_ref[2]
    out_ref[...] = (num * (1.0 / (ea + eb + ec))).astype(out_ref.dtype)


def _dilated(p, off_q, off_k, off_v, gq, gk, rope, batch, seq):
    ng = len(DIL_PATTERNS)
    n_strided = sum(dil > 1 for _, dil in DIL_PATTERNS)
    tables, shifts = rope
    kw_max = max(min(seq // dil, QBLOCK + window // dil) for window, dil in DIL_PATTERNS)
    kern = functools.partial(_dilated_kernel, seq=seq, patterns=DIL_PATTERNS, shifts=shifts)
    group_w = DIL_HPG * HEAD_DIM

    def head(off, g):
        assert off % HEAD_DIM == 0
        cb = (off + g * group_w) // HEAD_DIM
        return pl.BlockSpec((seq, HEAD_DIM), lambda b, h: (b, cb + h))

    const = lambda shape: pl.BlockSpec(shape, lambda b, h: (0, 0))
    slab = _nbytes((seq, HEAD_DIM), F32)
    return pl.pallas_call(
        kern,
        grid=(batch, DIL_HPG),
        in_specs=([head(off_q, g) for g in range(ng)] + [head(off_k, g) for g in range(ng)]
                  + [head(off_v, g) for g in range(ng)]
                  + [const((1, HEAD_DIM))] * 2 + [const((seq, LANES))] * len(tables)),
        out_specs=pl.BlockSpec((seq, HEAD_DIM), lambda b, h: (b, h)),
        out_shape=jax.ShapeDtypeStruct((batch * seq, group_w), BF16),
        scratch_shapes=[pltpu.VMEM((ng - n_strided, seq, HEAD_DIM), BF16)] * 2
                       + [pltpu.VMEM((n_strided, seq, HEAD_DIM), F32)] * 3
                       + [pltpu.VMEM((ng, seq, HEAD_DIM), F32)] * 2
                       + [pltpu.VMEM((seq // QBLOCK, QBLOCK, kw_max), F32),
                          pltpu.VMEM((seq // QBLOCK, QBLOCK, kw_max), BF16),
                          pltpu.VMEM((seq // QBLOCK, QBLOCK, HEAD_DIM), F32)],
        compiler_params=_params(
            ("parallel", "parallel"),
            (3 * ng + 1) * _nbytes((seq, HEAD_DIM), BF16), len(tables) * slab,
            scratch_bytes=(ng - n_strided + 3 * n_strided + 2 * ng) * slab + 4 * slab),
        name="dilated",
    )(*([p] * (3 * ng)), gq, gk, *tables)


def _merge_kernel(xn_ref, oa_ref, ob_ref, oc_ref, wga_ref, wgb_ref, wgc_ref, bga_ref, bgb_ref, bgc_ref,
                  wa_ref, wb_ref, wc_ref, o_ref):
    xn = xn_ref[...]
    merged = None
    for o_i, w_i, wg_i, bg_i in ((oa_ref, wa_ref, wga_ref, bga_ref), (ob_ref, wb_ref, wgb_ref, bgb_ref),
                                 (oc_ref, wc_ref, wgc_ref, bgc_ref)):
        logit = lax.dot_general(xn, wg_i[...], NT_DIMS, preferred_element_type=F32) + bg_i[...]
        gate = 1.0 / (1.0 + jnp.exp(-logit))
        term = gate * jnp.dot(o_i[...], w_i[...].astype(BF16), preferred_element_type=F32)
        merged = term if merged is None else merged + term
    o_ref[...] = merged.astype(o_ref.dtype)


def _merge(xn, oa, ob, oc, w_gate, b_gate, w_oa, w_ob, w_oc, layer, tm, tn):
    m, d = xn.shape
    nj = d // tn
    row = lambda a: pl.BlockSpec((tm, a.shape[1]), lambda i, j: (i, 0))
    wcol = lambda w: pl.BlockSpec((None, w.shape[1], tn), lambda i, j: (layer, 0, j))
    wgate = lambda g: pl.BlockSpec((tn, d), lambda i, j: (g * nj + j, 0))
    bgate = lambda g: pl.BlockSpec((1, tn), lambda i, j: (0, g * nj + j))
    k_branches = oa.shape[1] + ob.shape[1] + oc.shape[1]
    return pl.pallas_call(
        _merge_kernel,
        grid=(m // tm, nj),
        in_specs=[row(xn), row(oa), row(ob), row(oc), wgate(0), wgate(1), wgate(2),
                  bgate(0), bgate(1), bgate(2), wcol(w_oa), wcol(w_ob), wcol(w_oc)],
        out_specs=pl.BlockSpec((tm, tn), lambda i, j: (i, j)),
        out_shape=jax.ShapeDtypeStruct((m, d), BF16),
        compiler_params=_params(
            ("parallel", "parallel"),
            _nbytes((tm, d + k_branches), BF16), _nbytes((tm, tn), BF16),
            _nbytes((3 * d + k_branches, tn), BF16),
            scratch_bytes=6 * _nbytes((tm, tn), F32)),
        name="merge",
    )(xn, oa, ob, oc, w_gate, w_gate, w_gate, b_gate, b_gate, b_gate, w_oa, w_ob, w_oc)


def _resid_matmul_kernel(x_ref, a_ref, w_ref, o_ref):
    o_ref[...] = x_ref[...] + jnp.dot(a_ref[...], w_ref[...].astype(BF16),
                                      preferred_element_type=F32)


def _resid_matmul(x2d, a, w, layer, tm, tn):
    m, d = x2d.shape
    k = a.shape[1]
    return pl.pallas_call(
        _resid_matmul_kernel,
        grid=(m // tm, d // tn),
        in_specs=[
            pl.BlockSpec((tm, tn), lambda i, j: (i, j)),
            pl.BlockSpec((tm, k), lambda i, j: (i, 0)),
            pl.BlockSpec((None, k, tn), lambda i, j: (layer, 0, j)),
        ],
        out_specs=pl.BlockSpec((tm, tn), lambda i, j: (i, j)),
        out_shape=jax.ShapeDtypeStruct((m, d), F32),
        compiler_params=_params(
            ("parallel", "parallel"),
            2 * _nbytes((tm, tn), F32), _nbytes((tm, k), BF16), _nbytes((k, tn), BF16)),
        name="out_proj",
    )(x2d, a, w)


def _mlp_kernel(x_ref, g_ref, wu_ref, wd_ref, o_ref, xn_ref):
    c = pl.program_id(1)

    @pl.when(c == 0)
    def _():
        x = x_ref[...]
        ms = jnp.mean(x * x, axis=-1, keepdims=True)
        xn_ref[...] = (x * lax.rsqrt(ms + EPS) * g_ref[...]).astype(BF16)
        o_ref[...] = x

    h = jnp.dot(xn_ref[...], wu_ref[...].astype(BF16), preferred_element_type=F32)
    h = jnp.square(jnp.maximum(h, 0.0)).astype(BF16)
    o_ref[...] += jnp.dot(h, wd_ref[...].astype(BF16), preferred_element_type=F32)


def _mlp(x2d, gain, w_up, w_down, layer, tm, tf):
    m, d = x2d.shape
    f = w_up.shape[2]
    return pl.pallas_call(
        _mlp_kernel,
        grid=(m // tm, f // tf),
        in_specs=[
            pl.BlockSpec((tm, d), lambda i, c: (i, 0)),
            pl.BlockSpec((1, d), lambda i, c: (0, 0)),
            pl.BlockSpec((None, d, tf), lambda i, c: (layer, 0, c)),
            pl.BlockSpec((None, tf, d), lambda i, c: (layer, c, 0)),
        ],
        out_specs=pl.BlockSpec((tm, d), lambda i, c: (i, 0)),
        out_shape=jax.ShapeDtypeStruct((m, d), F32),
        scratch_shapes=[pltpu.VMEM((tm, d), BF16)],
        compiler_params=_params(
            ("parallel", "arbitrary"),
            2 * _nbytes((tm, d), F32), 2 * _nbytes((d, tf), F32),
            scratch_bytes=_nbytes((tm, d), BF16) + _nbytes((tm, tf), F32)),
        name="mlp",
    )(x2d, gain, w_up, w_down)


def _largest_tile(n, cap):
    t = min(n, cap)
    while n % t:
        t //= 2
    return t


def kernel(x, attn_norm, w_in, b_gate, mla_q_lat_norm, w_uq, mla_kv_lat_norm, w_ukv,
           mla_q_head_norm, mla_k_head_norm, gqa_q_norm, gqa_k_norm, dil_q_norm, dil_k_norm,
           w_oa, w_ob, w_oc, w_out, mlp_norm, w_up, w_down):
    batch, seq, d = x.shape
    depth = w_in.shape[0]
    m = batch * seq
    d_ff = w_up.shape[2]
    gate_cols = 3 * d
    assert w_in.shape[2] == A_COLS + B_COLS + C_COLS + gate_cols
    assert seq % GRID_W == 0 and seq % (16 * QBLOCK) == 0 and d % LANES == 0

    tn_proj = 1792
    nq_b, nk_b = GQA_HEADS * HEAD_DIM, GQA_KV_HEADS * HEAD_DIM
    nh_c = DIL_HEADS * HEAD_DIM
    off_bq = 0
    off_bk = off_bq + nq_b
    off_bv = off_bk + nk_b
    off_cq = off_bv + nk_b
    off_ck = off_cq + nh_c
    off_cv = off_ck + nh_c
    off_aq = off_cv + nh_c
    off_akv = off_aq + MLA_Q_RANK
    off_ape = off_akv + MLA_KV_RANK
    n_proj = _round_up(off_ape + LANES, tn_proj)

    tm_big = _largest_tile(m, 1024)
    tm_half = _largest_tile(m, 512)
    tm_seq = _largest_tile(seq, 1024)
    tm_mla = _largest_tile(seq, 512)

    pos = np.arange(seq, dtype=np.float64)
    row = np.repeat(np.arange(seq // GRID_W, dtype=np.float64), GRID_W)
    col = np.tile(np.arange(GRID_W, dtype=np.float64), seq // GRID_W)
    half_mla, half_ax, half_dil = MLA_ROPE // 2, HEAD_DIM // 4, PARTIAL_ROPE_DIM // 2
    rope_mla = _rope_tables(seq, half_mla, [(0, half_mla, pos, MLA_ROPE_THETA)])
    rope_gqa = _rope_tables(seq, half_ax, [(0, half_ax, row, AXIAL_THETA),
                                           (2 * half_ax, half_ax, col, AXIAL_THETA)])
    rope_dil = _rope_tables(seq, half_dil, [(0, half_dil, pos, PARTIAL_ROPE_THETA)])

    qk_pad = MLA_QK_PAD - MLA_QK
    w_in_t = jnp.swapaxes(w_in, 1, 2)
    c1 = A_COLS + B_COLS + C_COLS
    tr_w = math.gcd(c1 - A_COLS, gate_cols, 1024)
    a_blocks = -(-A_COLS // tr_w)
    starts_p = [A_COLS + r * tr_w for r in range((c1 - A_COLS) // tr_w)] + [k * tr_w for k in range(a_blocks)]
    valid_p = ([tr_w] * ((c1 - A_COLS) // tr_w)
               + [min(tr_w, A_COLS - k * tr_w) for k in range(a_blocks)])
    starts_g = [c1 + r * tr_w for r in range(gate_cols // tr_w)]
    assert len(starts_p) * tr_w == n_proj

    x2d = x.reshape(m, d)
    for l in range(depth):
        w_p = _gather_rows(w_in_t, l, starts_p, valid_p, tr_w)
        w_g = _gather_rows(w_in_t, l, starts_g, [tr_w] * len(starts_g), tr_w)
        p, xn = _proj(x2d, attn_norm[l][None, :], w_p, tm_big, tn_proj)

        wq = jnp.pad(w_uq[l].reshape(MLA_Q_RANK, MLA_HEADS, MLA_QK), ((0, 0), (0, 0), (0, qk_pad))
                     ).reshape(MLA_Q_RANK, MLA_HEADS * MLA_QK_PAD).astype(BF16)
        wkv = w_ukv[l].reshape(MLA_KV_RANK, MLA_HEADS, MLA_NOPE + MLA_V)
        wk = wkv[:, :, :MLA_NOPE].reshape(MLA_KV_RANK, MLA_HEADS * MLA_NOPE).astype(BF16)
        wv = wkv[:, :, MLA_NOPE:].reshape(MLA_KV_RANK, MLA_HEADS * MLA_V).astype(BF16)
        gqh = jnp.pad(mla_q_head_norm[l] * (LOG2_E * MLA_QK ** -0.5), (0, qk_pad))[None, :]
        gkn = mla_k_head_norm[l][None, :MLA_NOPE]
        gkr = jnp.pad(mla_k_head_norm[l][MLA_NOPE:], (0, LANES - MLA_ROPE))[None, :]
        qa, ka, va = _mla_prep(p, off_aq, off_akv, off_ape, wq, wk, wv,
                               mla_q_lat_norm[l][None, :], mla_kv_lat_norm[l][None, :],
                               gqh, gkn, gkr, rope_mla, seq, tm_mla)
        oa = _attention(qa, ka, va, 0, 0, 0, MLA_HEADS, 1, MLA_QK_PAD, MLA_V, batch, seq)

        scale = HEAD_DIM ** -0.5
        gq = jnp.tile(gqa_q_norm[l] * (LOG2_E * scale), GQA_HEADS)[None, :]
        gk = jnp.tile(gqa_k_norm[l], GQA_KV_HEADS)[None, :]
        qb = _headnorm_rope(p, off_bq, nq_b, gq, rope_gqa, seq, tm_seq)
        kb = _headnorm_rope(p, off_bk, nk_b, gk, rope_gqa, seq, tm_seq)
        ob = _attention(qb, kb, p, 0, 0, off_bv, GQA_HEADS, GQA_GROUP, HEAD_DIM, HEAD_DIM,
                        batch, seq)

        oc = _dilated(p, off_cq, off_ck, off_cv, (dil_q_norm[l] * scale)[None, :],
                      dil_k_norm[l][None, :], rope_dil, batch, seq)

        merged = _merge(xn, oa, ob, oc, w_g, b_gate[l][None, :], w_oa, w_ob, w_oc, l,
                        tm_big, _largest_tile(d, 512))
        x2d = _resid_matmul(x2d, merged, w_out, l, tm_half, d)
        x2d = _mlp(x2d, mlp_norm[l][None, :], w_up, w_down, l,
                   _largest_tile(m, 1024), _largest_tile(d_ff, 512))
    return x2d.reshape(batch, seq, d)
```
